```python
import jax, jax.numpy as jnp
from jax import lax
import numpy as np

D_MODEL = 1024
BATCH = 8
SEQ = 2048
DEPTH = 1

N_MEM = 256
SB_HEADS = 8
SB_HEAD_DIM = 64
DSA_HEADS = 8
DSA_KV_HEADS = 2
DSA_HEAD_DIM = 64
IDX_HEADS = 8
IDX_HEAD_DIM = 64
TOPK_MAX = 256
MEM_HEADS = 4
MEM_HEAD_DIM = 128
D_FF = 4 * D_MODEL
Q_BLOCK = 128
ROPE_THETA = 10000.0
EPS = 1e-6

IN_WIDTHS = (
    SB_HEADS * SB_HEAD_DIM,
    SB_HEADS * SB_HEAD_DIM,
    SB_HEADS * SB_HEAD_DIM,
    DSA_HEADS * DSA_HEAD_DIM,
    DSA_KV_HEADS * DSA_HEAD_DIM,
    DSA_KV_HEADS * DSA_HEAD_DIM,
    IDX_HEADS * IDX_HEAD_DIM,
    IDX_HEAD_DIM,
    IDX_HEADS,
)
D_IN = int(sum(IN_WIDTHS))
IN_OFFSETS = [int(o) for o in np.cumsum(IN_WIDTHS)[:-1]]

kernel_name = "hybrid_stickbreak_dsa_gated_block"


def _rmsnorm(x, g):
    x32 = x.astype(jnp.float32)
    y = x32 * lax.rsqrt(jnp.mean(x32 * x32, axis=-1, keepdims=True) + EPS)
    return (y * g.astype(jnp.float32)).astype(x.dtype)


def _rope(x, pos):
    d = x.shape[-1]
    inv = ROPE_THETA ** (-jnp.arange(0, d, 2, dtype=jnp.float32) / d)
    ang = pos.astype(jnp.float32)[:, None] * inv[None, :]
    cos = jnp.cos(ang)[None, :, None, :]
    sin = jnp.sin(ang)[None, :, None, :]
    x32 = x.astype(jnp.float32)
    x1, x2 = x32[..., : d // 2], x32[..., d // 2:]
    out = jnp.concatenate([x1 * cos - x2 * sin, x2 * cos + x1 * sin], axis=-1)
    return out.astype(x.dtype)


def _to_blocks(a):
    b, s = a.shape[:2]
    return jnp.moveaxis(a.reshape(b, s // Q_BLOCK, Q_BLOCK, *a.shape[2:]), 1, 0)


def _from_blocks(a):
    nb, b, qb = a.shape[:3]
    return jnp.moveaxis(a, 0, 1).reshape(b, nb * qb, *a.shape[3:])


def _stick_breaking_attention(q, k, v):
    b, s, h, dh = q.shape
    scale = dh ** -0.5
    key_pos = jnp.arange(s)
    t_blocks = jnp.arange(s).reshape(s // Q_BLOCK, Q_BLOCK)

    def block(args):
        qb, t_idx = args
        z = jnp.einsum('bqhd,bkhd->bhqk', qb, k).astype(jnp.float32) * scale
        strict = (key_pos[None, :] < t_idx[:, None])[None, None]
        log_beta = jax.nn.log_sigmoid(z)
        log_1mb = jnp.where(strict, jax.nn.log_sigmoid(-z), 0.0)
        suffix = lax.cumsum(log_1mb, axis=3, reverse=True) - log_1mb
        a = jnp.where(strict, jnp.exp(log_beta + suffix), 0.0)
        return jnp.einsum('bhqk,bkhd->bqhd', a.astype(v.dtype), v)

    out = lax.map(block, (_to_blocks(q), t_blocks))
    return _from_blocks(out).reshape(b, s, h * dh)


def _dsa_attention(q, k, v, iq, ik, iw):
    b, s, h, dh = q.shape
    g = k.shape[2]
    r = h // g
    k_top = min(TOPK_MAX, s // 4)
    scale = dh ** -0.5
    idx_scale = iq.shape[-1] ** -0.5
    w_scale = iw.shape[-1] ** -0.5
    key_pos = jnp.arange(s)
    t_blocks = jnp.arange(s).reshape(s // Q_BLOCK, Q_BLOCK)

    def block(args):
        qb, iqb, iwb, t_idx = args
        dots = jnp.einsum('bqhd,bkd->bqhk', iqb, ik).astype(jnp.float32) * idx_scale
        score = jnp.einsum('bqhk,bqh->bqk', jax.nn.relu(dots),
                           iwb.astype(jnp.float32) * w_scale)
        causal = (key_pos[None, :] <= t_idx[:, None])[None]
        score = jnp.where(causal, score, -jnp.inf)
        vals, sel = lax.top_k(score, k_top)
        valid = jnp.isfinite(vals)
        kg = jax.vmap(lambda kk, ii: kk[ii])(k, sel)
        vg = jax.vmap(lambda vv, ii: vv[ii])(v, sel)
        qg = qb.reshape(b, Q_BLOCK, g, r, dh)
        logits = jnp.einsum('bqgrd,bqkgd->bqgrk', qg, kg).astype(jnp.float32) * scale
        logits = jnp.where(valid[:, :, None, None, :], logits, -jnp.inf)
        p = jax.nn.softmax(logits, axis=-1)
        o = jnp.einsum('bqgrk,bqkgd->bqgrd', p.astype(vg.dtype), vg)
        return o.reshape(b, Q_BLOCK, h * dh)

    out = lax.map(block, (_to_blocks(q), _to_blocks(iq), _to_blocks(iw), t_blocks))
    return _from_blocks(out)


def _token_mixer(u, w_in, w_branch_sb, w_branch_dsa, w_gate, b_gate, w_out):
    b, s, _ = u.shape
    pos = jnp.arange(s)
    proj = u @ w_in
    sb_q, sb_k, sb_v, dq, dk, dv, iq, ik, iw = jnp.split(proj, IN_OFFSETS, axis=-1)
    sb_q = sb_q.reshape(b, s, SB_HEADS, SB_HEAD_DIM)
    sb_k = sb_k.reshape(b, s, SB_HEADS, SB_HEAD_DIM)
    sb_v = sb_v.reshape(b, s, SB_HEADS, SB_HEAD_DIM)
    dq = _rope(dq.reshape(b, s, DSA_HEADS, DSA_HEAD_DIM), pos)
    dk = _rope(dk.reshape(b, s, DSA_KV_HEADS, DSA_HEAD_DIM), pos)
    dv = dv.reshape(b, s, DSA_KV_HEADS, DSA_HEAD_DIM)
    iq = _rope(iq.reshape(b, s, IDX_HEADS, IDX_HEAD_DIM), pos)
    ik = _rope(ik[:, :, None, :], pos)[:, :, 0, :]

    o_sb = _stick_breaking_attention(sb_q, sb_k, sb_v)
    o_dsa = _dsa_attention(dq, dk, dv, iq, ik, iw)

    gates = jax.nn.sigmoid((u @ w_gate + b_gate).astype(jnp.float32)).astype(u.dtype)
    g_sb, g_dsa = jnp.split(gates, 2, axis=-1)
    merged = g_sb * (o_sb @ w_branch_sb) + g_dsa * (o_dsa @ w_branch_dsa)
    return merged @ w_out


def _memory_cross_attention(u, mem_n, w_cq, w_ckv, w_co):
    b, s, _ = u.shape
    m = mem_n.shape[1]
    q = (u @ w_cq).reshape(b, s, MEM_HEADS, MEM_HEAD_DIM)
    k, v = jnp.split(mem_n @ w_ckv, 2, axis=-1)
    k = k.reshape(b, m, MEM_HEADS, MEM_HEAD_DIM)
    v = v.reshape(b, m, MEM_HEADS, MEM_HEAD_DIM)
    logits = jnp.einsum('bqhd,bmhd->bhqm', q, k).astype(jnp.float32) * MEM_HEAD_DIM ** -0.5
    p = jax.nn.softmax(logits, axis=-1)
    o = jnp.einsum('bhqm,bmhd->bqhd', p.astype(v.dtype), v)
    return o.reshape(b, s, MEM_HEADS * MEM_HEAD_DIM) @ w_co


def _squared_relu_mlp(u, w_up, w_down):
    hdn = jax.nn.relu(u @ w_up)
    return (hdn * hdn) @ w_down


def setup_inputs(seed: int = 0) -> dict:
    key = jax.random.key(seed)
    ks = jax.random.split(key, 20)

    def dense(k, fan_in, fan_out):
        return jax.random.normal(k, (DEPTH, fan_in, fan_out), jnp.float32) * fan_in ** -0.5

    def gain(k, n, depth=True):
        shape = (DEPTH, n) if depth else (n,)
        return 1.0 + 0.02 * jax.random.normal(k, shape, jnp.float32)

    return {
        "x": jax.random.normal(ks[0], (BATCH, SEQ, D_MODEL), jnp.float32),
        "mem": jax.random.normal(ks[1], (BATCH, N_MEM, D_MODEL), jnp.float32),
        "norm_mix": gain(ks[2], D_MODEL),
        "w_in": dense(ks[3], D_MODEL, D_IN),
        "w_branch_sb": dense(ks[4], SB_HEADS * SB_HEAD_DIM, D_MODEL),
        "w_branch_dsa": dense(ks[5], DSA_HEADS * DSA_HEAD_DIM, D_MODEL),
        "w_gate": dense(ks[6], D_MODEL, 2 * D_MODEL),
        "b_gate": 0.01 * jax.random.normal(ks[7], (DEPTH, 2 * D_MODEL), jnp.float32),
        "w_out": dense(ks[8], D_MODEL, D_MODEL),
        "norm_cross": gain(ks[9], D_MODEL),
        "norm_mem": gain(ks[10], D_MODEL),
        "w_cq": dense(ks[11], D_MODEL, MEM_HEADS * MEM_HEAD_DIM),
        "w_ckv": dense(ks[12], D_MODEL, 2 * MEM_HEADS * MEM_HEAD_DIM),
        "w_co": dense(ks[13], MEM_HEADS * MEM_HEAD_DIM, D_MODEL),
        "norm_mlp": gain(ks[14], D_MODEL),
        "w_up": dense(ks[15], D_MODEL, D_FF),
        "w_down": dense(ks[16], D_FF, D_MODEL),
        "norm_final": gain(ks[17], D_MODEL, depth=False),
    }


def reference(x, mem, norm_mix, w_in, w_branch_sb, w_branch_dsa, w_gate, b_gate, w_out,
              norm_cross, norm_mem, w_cq, w_ckv, w_co, norm_mlp, w_up, w_down, norm_final):
    h = x
    for l in range(DEPTH):
        u = _rmsnorm(h, norm_mix[l])
        h = h + _token_mixer(u, w_in[l], w_branch_sb[l], w_branch_dsa[l],
                             w_gate[l], b_gate[l], w_out[l])
        u = _rmsnorm(h, norm_cross[l])
        mem_n = _rmsnorm(mem, norm_mem[l])
        h = h + _memory_cross_attention(u, mem_n, w_cq[l], w_ckv[l], w_co[l])
        u = _rmsnorm(h, norm_mlp[l])
        h = h + _squared_relu_mlp(u, w_up[l], w_down[l])
    return _rmsnorm(h, norm_final)
```

```python
import functools

import jax
import jax.numpy as jnp
import numpy as np
from jax import lax
from jax.experimental import pallas as pl
from jax.experimental.pallas import tpu as pltpu

F32 = jnp.float32
BF16 = jnp.bfloat16

D_MODEL = 1024
N_MEM = 256
SB_HEADS = 8
DSA_HEADS = 8
DSA_KV_HEADS = 2
IDX_HEADS = 8
HEAD_DIM = 64
TOPK_MAX = 256
MEM_HEADS = 4
MEM_HEAD_DIM = 128
D_FF = 4 * D_MODEL
ROPE_THETA = 10000.0
EPS = 1e-6

LANES = 128
HALF_ROT = HEAD_DIM // 2

W_SB = SB_HEADS * HEAD_DIM
OFF_SBQ = 0
OFF_SBK = OFF_SBQ + W_SB
OFF_SBV = OFF_SBK + W_SB
OFF_DQ = OFF_SBV + W_SB
OFF_DK = OFF_DQ + DSA_HEADS * HEAD_DIM
OFF_DV = OFF_DK + DSA_KV_HEADS * HEAD_DIM
OFF_IQ = OFF_DV + DSA_KV_HEADS * HEAD_DIM
OFF_IK = OFF_IQ + IDX_HEADS * HEAD_DIM
OFF_IW = OFF_IK + LANES
D_IN_PACKED = OFF_IW + LANES

ROW_TILE = 512
SB_TILE = 256
DSA_TQ = 128
DSA_KCHUNK = 512
VMEM_LIMIT = 56 * 1024 * 1024
NEG_BIG = -1e30
INT_MIN = -(2 ** 31)


def _rms(x, g):
    return x * lax.rsqrt(jnp.mean(x * x, axis=-1, keepdims=True) + EPS) * g


def _dot(a, b):
    return jnp.dot(a, b, preferred_element_type=F32)


def _dot_t(a, b):
    return lax.dot_general(a, b, (((1,), (1,)), ((), ())), preferred_element_type=F32)


def _proj_kernel(x_ref, g_ref, w_ref, cos_ref, sin_ref,
                 sbq_ref, sbk_ref, sbv_ref, dq_ref, dk_ref, dv_ref, iq_ref, ik_ref, iw_ref):
    ub = _rms(x_ref[...], g_ref[...]).astype(BF16)
    cos = cos_ref[...]
    sin = sin_ref[...]
    lane = lax.broadcasted_iota(jnp.int32, (1, LANES), 1)
    first_half = (lane % HEAD_DIM) < HALF_ROT

    def mm(off, width=LANES):
        return _dot(ub, w_ref[:, off:off + width])

    def rope(s):
        rot = jnp.where(first_half, pltpu.roll(s, LANES - HALF_ROT, 1), pltpu.roll(s, HALF_ROT, 1))
        return s * cos + rot * sin

    q_scale = HEAD_DIM ** -0.5
    sbq_ref[...] = (mm(OFF_SBQ, W_SB) * q_scale).astype(BF16)
    sbk_ref[...] = mm(OFF_SBK, W_SB).astype(BF16)
    sbv_ref[...] = mm(OFF_SBV, W_SB).astype(BF16)
    for p in range(4):
        sl = slice(p * LANES, (p + 1) * LANES)
        dq_ref[:, sl] = (rope(mm(OFF_DQ + p * LANES)) * q_scale).astype(BF16)
        iq_ref[:, sl] = (rope(mm(OFF_IQ + p * LANES)) * q_scale).astype(BF16)
    dk_ref[...] = rope(mm(OFF_DK)).astype(BF16)
    dv_ref[...] = mm(OFF_DV).astype(BF16)
    ik_ref[...] = rope(mm(OFF_IK)).astype(BF16)
    iw_ref[...] = mm(OFF_IW) * (IDX_HEADS ** -0.5)


def _proj_call(x2, g, w_packed, cos_t, sin_t, seq):
    n = x2.shape[0]
    tm = ROW_TILE
    tiles_per_seq = seq // tm
    row = lambda i: (i, 0)
    const = lambda i: (0, 0)
    pos = lambda i: (i % tiles_per_seq, 0)
    out_w = [W_SB, W_SB, W_SB, 512, LANES, LANES, 512, LANES, LANES]
    out_dt = [BF16] * 8 + [F32]
    return pl.pallas_call(
        _proj_kernel,
        grid=(n // tm,),
        in_specs=[
            pl.BlockSpec((tm, D_MODEL), row),
            pl.BlockSpec((1, D_MODEL), const),
            pl.BlockSpec((D_MODEL, D_IN_PACKED), const),
            pl.BlockSpec((tm, LANES), pos),
            pl.BlockSpec((tm, LANES), pos),
        ],
        out_specs=[pl.BlockSpec((tm, w), row) for w in out_w],
        out_shape=[jax.ShapeDtypeStruct((n, w), dt) for w, dt in zip(out_w, out_dt)],
        compiler_params=pltpu.CompilerParams(
            dimension_semantics=("arbitrary",), vmem_limit_bytes=VMEM_LIMIT),
        name="proj",
    )(x2, g, w_packed, cos_t, sin_t)


def _sb_kernel(q_ref, k_ref, v_ref, o_ref):
    t = SB_TILE
    i = pl.program_id(2)
    q = q_ref[...]
    lane = lax.broadcasted_iota(jnp.int32, (1, LANES), 1)
    row = lax.broadcasted_iota(jnp.int32, (t, t), 0)
    col = lax.broadcasted_iota(jnp.int32, (t, t), 1)
    strict = col < row
    later = jnp.where(row > col, 1.0, 0.0).astype(BF16)
    zero_q = jnp.zeros_like(q)
    qh = [jnp.where((lane // HEAD_DIM) == h, q, zero_q) for h in range(2)]

    def tile(start, diag, qm, carry, acc):
        kb = k_ref[pl.ds(start, t), :]
        vb = v_ref[pl.ds(start, t), :]
        z = _dot_t(qm, kb)
        sp = jnp.log1p(jnp.exp(-jnp.abs(z)))
        log_beta = jnp.minimum(z, 0.0) - sp
        log_1mb = jnp.minimum(-z, 0.0) - sp
        if diag:
            log_1mb = jnp.where(strict, log_1mb, 0.0)
        hi = log_1mb.astype(BF16)
        lo = (log_1mb - hi.astype(F32)).astype(BF16)
        suffix = _dot(hi, later) + _dot(lo, later) + carry
        a = jnp.exp(log_beta + suffix)
        if diag:
            a = jnp.where(strict, a, 0.0)
        acc = acc + _dot(a.astype(BF16), vb)
        carry = carry + jnp.sum(log_1mb, axis=1, keepdims=True)
        return carry, acc

    zc = jnp.zeros((t, 1), F32)
    za = jnp.zeros((t, LANES), F32)
    diag_start = pl.multiple_of(i * t, t)
    state = []
    for h in range(2):
        state.extend(tile(diag_start, True, qh[h], zc, za))

    def body(s, st):
        start = pl.multiple_of((i - 1 - s) * t, t)
        c0, a0 = tile(start, False, qh[0], st[0], st[1])
        c1, a1 = tile(start, False, qh[1], st[2], st[3])
        return (c0, a0, c1, a1)

    st = lax.fori_loop(0, i, body, tuple(state))
    o_ref[...] = jnp.where(lane < HEAD_DIM, st[1], st[3]).astype(BF16)


def _sb_call(q, k, v, batch, seq):
    t = SB_TILE
    nq = seq // t
    pairs = W_SB // LANES
    qmap = lambda b, p, i: (b * nq + i, p)
    kvmap = lambda b, p, i: (b, p)
    return pl.pallas_call(
        _sb_kernel,
        grid=(batch, pairs, nq),
        in_specs=[
            pl.BlockSpec((t, LANES), qmap),
            pl.BlockSpec((seq, LANES), kvmap),
            pl.BlockSpec((seq, LANES), kvmap),
        ],
        out_specs=pl.BlockSpec((t, LANES), qmap),
        out_shape=jax.ShapeDtypeStruct((batch * seq, W_SB), BF16),
        compiler_params=pltpu.CompilerParams(
            dimension_semantics=("arbitrary", "arbitrary", "arbitrary"),
            vmem_limit_bytes=VMEM_LIMIT),
        name="stickbreak",
    )(q, k, v)


def _dsa_kernel(dq_ref, iq_ref, iw_ref, dk_ref, dv_ref, ik_ref, o_ref, key_sc, bias_sc, *, seq):
    tq = DSA_TQ
    i = pl.program_id(1)
    lane = lax.broadcasted_iota(jnp.int32, (1, LANES), 1)
    half = lane // HEAD_DIM
    t_idx = i * tq + lax.broadcasted_iota(jnp.int32, (tq, 1), 0)

    iq = iq_ref[...]
    zero_q = jnp.zeros((tq, LANES), BF16)
    qstack = jnp.concatenate(
        [jnp.where(half == (h % 2), iq[:, (h // 2) * LANES:(h // 2 + 1) * LANES], zero_q)
         for h in range(IDX_HEADS)], axis=0)
    iw = iw_ref[...]
    wcol = [iw[:, h:h + 1] for h in range(IDX_HEADS)]
    for c in range(seq // DSA_KCHUNK):
        ksl = slice(c * DSA_KCHUNK, (c + 1) * DSA_KCHUNK)
        dots = _dot_t(qstack, ik_ref[ksl, :])
        score = jnp.zeros((tq, DSA_KCHUNK), F32)
        for h in range(IDX_HEADS):
            score = score + jnp.maximum(dots[h * tq:(h + 1) * tq], 0.0) * wcol[h]
        kpos = c * DSA_KCHUNK + lax.broadcasted_iota(jnp.int32, (1, DSA_KCHUNK), 1)
        score = jnp.where(kpos <= t_idx, score, -jnp.inf)
        bits = lax.bitcast_convert_type(score, jnp.int32)
        key_sc[:, ksl] = bits ^ ((bits >> 31) & 0x7FFFFFFF)

    k_row = jnp.minimum(t_idx + 1, TOPK_MAX).astype(F32)

    def count_ge(cand):
        return jnp.sum(jnp.where(key_sc[...] >= cand, 1.0, 0.0), axis=1, keepdims=True)

    thr = jnp.where(count_ge(jnp.zeros((tq, 1), jnp.int32)) >= k_row, 0, INT_MIN).astype(jnp.int32)

    def bisect(it, thr):
        cand = thr + jnp.left_shift(jnp.int32(1), 30 - it)
        return jnp.where(count_ge(cand) >= k_row, cand, thr)

    thr = lax.fori_loop(0, 31, bisect, thr)

    n_gt = jnp.sum(jnp.where(key_sc[...] > thr, 1.0, 0.0), axis=1, keepdims=True)
    need = k_row - n_gt
    r128 = lax.broadcasted_iota(jnp.int32, (LANES, LANES), 0)
    c128 = lax.broadcasted_iota(jnp.int32, (LANES, LANES), 1)
    earlier = jnp.where(r128 < c128, 1.0, 0.0).astype(BF16)
    ties_before = jnp.zeros((tq, 1), F32)
    for c in range(seq // LANES):
        ksl = slice(c * LANES, (c + 1) * LANES)
        kc = key_sc[:, ksl]
        eq = kc == thr
        eqf = jnp.where(eq, 1.0, 0.0)
        rank = _dot(eqf.astype(BF16), earlier) + ties_before
        tie_bias = jnp.where(rank < need, 0.0, NEG_BIG)
        bias_sc[:, ksl] = jnp.where(kc > thr, 0.0, jnp.where(eq, tie_bias, NEG_BIG))
        ties_before = ties_before + jnp.sum(eqf, axis=1, keepdims=True)

    dq = dq_ref[...]
    dk = dk_ref[...]
    dv = dv_ref[...]
    for p in range(DSA_HEADS // 2):
        slab = dq[:, p * LANES:(p + 1) * LANES]
        outs = []
        for g in range(DSA_KV_HEADS):
            qh = jnp.where(half == g, slab, zero_q)
            logits = _dot_t(qh, dk) + bias_sc[...]
            m = jnp.max(logits, axis=1, keepdims=True)
            pexp = jnp.exp(logits - m)
            denom = jnp.sum(pexp, axis=1, keepdims=True)
            outs.append(_dot(pexp.astype(BF16), dv) / denom)
        o_ref[:, p * LANES:(p + 1) * LANES] = jnp.where(half == 0, outs[0], outs[1]).astype(BF16)


def _dsa_call(dq, iq, iw, dk, dv, ik, batch, seq):
    tq = DSA_TQ
    nq = seq // tq
    qmap = lambda b, i: (b * nq + i, 0)
    kvmap = lambda b, i: (b, 0)
    return pl.pallas_call(
        functools.partial(_dsa_kernel, seq=seq),
        grid=(batch, nq),
        in_specs=[
            pl.BlockSpec((tq, 512), qmap),
            pl.BlockSpec((tq, 512), qmap),
            pl.BlockSpec((tq, LANES), qmap),
            pl.BlockSpec((seq, LANES), kvmap),
            pl.BlockSpec((seq, LANES), kvmap),
            pl.BlockSpec((seq, LANES), kvmap),
        ],
        out_specs=pl.BlockSpec((tq, 512), qmap),
        out_shape=jax.ShapeDtypeStruct((batch * seq, 512), BF16),
        scratch_shapes=[pltpu.VMEM((tq, seq), jnp.int32), pltpu.VMEM((tq, seq), F32)],
        compiler_params=pltpu.CompilerParams(
            dimension_semantics=("arbitrary", "arbitrary"), vmem_limit_bytes=VMEM_LIMIT),
        name="dsa",
    )(dq, iq, iw, dk, dv, ik)


def _memkv_kernel(m_ref, g_ref, w_ref, k_ref, v_ref):
    mb = _rms(m_ref[...], g_ref[...]).astype(BF16)
    kv = _dot(mb, w_ref[...])
    half = MEM_HEADS * MEM_HEAD_DIM
    k_ref[...] = kv[:, :half].astype(BF16)
    v_ref[...] = kv[:, half:].astype(BF16)


def _memkv_call(mem2, g, w_ckv, batch):
    half = MEM_HEADS * MEM_HEAD_DIM
    row = lambda b: (b, 0)
    const = lambda b: (0, 0)
    return pl.pallas_call(
        _memkv_kernel,
        grid=(batch,),
        in_specs=[
            pl.BlockSpec((N_MEM, D_MODEL), row),
            pl.BlockSpec((1, D_MODEL), const),
            pl.BlockSpec((D_MODEL, 2 * half), const),
        ],
        out_specs=[pl.BlockSpec((N_MEM, half), row)] * 2,
        out_shape=[jax.ShapeDtypeStruct((batch * N_MEM, half), BF16)] * 2,
        compiler_params=pltpu.CompilerParams(
            dimension_semantics=("arbitrary",), vmem_limit_bytes=VMEM_LIMIT),
        name="memkv",
    )(mem2, g, w_ckv)


def _mix_kernel(x_ref, osb_ref, odsa_ref, g1_ref, wg_ref, bg_ref, wbs_ref, wbd_ref, wo_ref,
                g2_ref, wcq_ref, km_ref, vm_ref, wco_ref, h_ref):
    x = x_ref[...]
    ub = _rms(x, g1_ref[...]).astype(BF16)
    gates = 1.0 / (1.0 + jnp.exp(-(_dot(ub, wg_ref[...]) + bg_ref[...])))
    merged = (gates[:, :D_MODEL] * _dot(osb_ref[...], wbs_ref[...])
              + gates[:, D_MODEL:] * _dot(odsa_ref[...], wbd_ref[...]))
    h1 = x + _dot(merged.astype(BF16), wo_ref[...])

    u2 = _rms(h1, g2_ref[...]).astype(BF16)
    qb = _dot(u2, wcq_ref[...]).astype(BF16)
    km = km_ref[...]
    vm = vm_ref[...]
    outs = []
    for h in range(MEM_HEADS):
        sl = slice(h * MEM_HEAD_DIM, (h + 1) * MEM_HEAD_DIM)
        logits = _dot_t(qb[:, sl], km[:, sl]) * (MEM_HEAD_DIM ** -0.5)
        m = jnp.max(logits, axis=1, keepdims=True)
        pexp = jnp.exp(logits - m)
        denom = jnp.sum(pexp, axis=1, keepdims=True)
        outs.append((_dot(pexp.astype(BF16), vm[:, sl]) / denom).astype(BF16))
    o = jnp.concatenate(outs, axis=1)
    h_ref[...] = h1 + _dot(o, wco_ref[...])


def _mix_call(x2, osb, odsa, g1, wg, bg, wbs, wbd, wo, g2, wcq, km, vm, wco, seq):
    n = x2.shape[0]
    tm = ROW_TILE
    tiles_per_seq = seq // tm
    half = MEM_HEADS * MEM_HEAD_DIM
    row = lambda i: (i, 0)
    const = lambda i: (0, 0)
    bat = lambda i: (i // tiles_per_seq, 0)

    def full(a):
        return pl.BlockSpec(a.shape, const)

    return pl.pallas_call(
        _mix_kernel,
        grid=(n // tm,),
        in_specs=[
            pl.BlockSpec((tm, D_MODEL), row),
            pl.BlockSpec((tm, 512), row),
            pl.BlockSpec((tm, 512), row),
            full(g1), full(wg), full(bg), full(wbs), full(wbd), full(wo),
            full(g2), full(wcq),
            pl.BlockSpec((N_MEM, half), bat),
            pl.BlockSpec((N_MEM, half), bat),
            full(wco),
        ],
        out_specs=pl.BlockSpec((tm, D_MODEL), row),
        out_shape=jax.ShapeDtypeStruct((n, D_MODEL), F32),
        compiler_params=pltpu.CompilerParams(
            dimension_semantics=("arbitrary",), vmem_limit_bytes=VMEM_LIMIT),
        name="mix",
    )(x2, osb, odsa, g1, wg, bg, wbs, wbd, wo, g2, wcq, km, vm, wco)


def _mlp_kernel(h_ref, g_ref, wu_ref, wd_ref, gf_ref, o_ref, *, final_norm):
    h = h_ref[...]
    ub = _rms(h, g_ref[...]).astype(BF16)
    acc = h
    for c in range(D_FF // D_MODEL):
        sl = slice(c * D_MODEL, (c + 1) * D_MODEL)
        hid = jnp.maximum(_dot(ub, wu_ref[:, sl]), 0.0)
        acc = acc + _dot((hid * hid).astype(BF16), wd_ref[sl, :])
    if final_norm:
        acc = _rms(acc, gf_ref[...])
    o_ref[...] = acc


def _mlp_call(h2, g, wu, wd, gf, final_norm):
    n = h2.shape[0]
    tm = ROW_TILE
    row = lambda i: (i, 0)
    const = lambda i: (0, 0)
    return pl.pallas_call(
        functools.partial(_mlp_kernel, final_norm=final_norm),
        grid=(n // tm,),
        in_specs=[
            pl.BlockSpec((tm, D_MODEL), row),
            pl.BlockSpec((1, D_MODEL), const),
            pl.BlockSpec((D_MODEL, D_FF), const),
            pl.BlockSpec((D_FF, D_MODEL), const),
            pl.BlockSpec((1, D_MODEL), const),
        ],
        out_specs=pl.BlockSpec((tm, D_MODEL), row),
        out_shape=jax.ShapeDtypeStruct((n, D_MODEL), F32),
        compiler_params=pltpu.CompilerParams(
            dimension_semantics=("arbitrary",), vmem_limit_bytes=VMEM_LIMIT),
        name="mlp",
    )(h2, g, wu, wd, gf)


def _pack_w_in(w_in):
    offs = np.cumsum([0, 512, 512, 512, 512, 128, 128, 512, 64, 8])
    sbq, sbk, sbv, dq, dk, dv, iq, ik, iw = [w_in[:, offs[j]:offs[j + 1]] for j in range(9)]
    dq = dq.reshape(D_MODEL, 2, 4, HEAD_DIM).transpose(0, 2, 1, 3).reshape(D_MODEL, 512)
    iw = jnp.pad(iw, ((0, 0), (0, LANES - IDX_HEADS)))
    return jnp.concatenate([sbq, sbk, sbv, dq, dk, dv, iq, ik, ik, iw], axis=1).astype(BF16)


def _rope_tables(seq):
    inv = ROPE_THETA ** (-jnp.arange(0, HEAD_DIM, 2, dtype=F32) / HEAD_DIM)
    ang = jnp.arange(seq).astype(F32)[:, None] * inv[None, :]
    cos = jnp.cos(ang)
    sin = jnp.sin(ang)
    cos_t = jnp.tile(cos, (1, LANES // HALF_ROT))
    sin_t = jnp.tile(jnp.concatenate([-sin, sin], axis=1), (1, LANES // HEAD_DIM))
    return cos_t, sin_t


def kernel(x, mem, norm_mix, w_in, w_branch_sb, w_branch_dsa, w_gate, b_gate, w_out, norm_cross,
           norm_mem, w_cq, w_ckv, w_co, norm_mlp, w_up, w_down, norm_final):
    batch, seq, d = x.shape
    depth = w_in.shape[0]
    cos_t, sin_t = _rope_tables(seq)
    h = x.reshape(batch * seq, d)
    mem2 = mem.reshape(batch * mem.shape[1], d)
    gf = norm_final.reshape(1, d)
    for l in range(depth):
        w_packed = _pack_w_in(w_in[l])
        wbd = (w_branch_dsa[l].reshape(2, 4, HEAD_DIM, d).transpose(1, 0, 2, 3)
               .reshape(DSA_HEADS * HEAD_DIM, d).astype(BF16))
        sbq, sbk, sbv, dq, dk, dv, iq, ik, iw = _proj_call(
            h, norm_mix[l].reshape(1, d), w_packed, cos_t, sin_t, seq)
        o_sb = _sb_call(sbq, sbk, sbv, batch, seq)
        o_dsa = _dsa_call(dq, iq, iw, dk, dv, ik, batch, seq)
        km, vm = _memkv_call(mem2, norm_mem[l].reshape(1, d), w_ckv[l].astype(BF16), batch)
        h2 = _mix_call(h, o_sb, o_dsa, norm_mix[l].reshape(1, d), w_gate[l].astype(BF16),
                       b_gate[l].reshape(1, -1), w_branch_sb[l].astype(BF16), wbd,
                       w_out[l].astype(BF16), norm_cross[l].reshape(1, d), w_cq[l].astype(BF16),
                       km, vm, w_co[l].astype(BF16), seq)
        h = _mlp_call(h2, norm_mlp[l].reshape(1, d), w_up[l].astype(BF16), w_down[l].astype(BF16),
                      gf, final_norm=(l == depth - 1))
    return h.reshape(batch, seq, d)
```

```python
import functools

import jax
import jax.numpy as jnp
import numpy as np
from jax import lax
from jax.experimental import pallas as pl
from jax.experimental.pallas import tpu as pltpu

F32 = jnp.float32
BF16 = jnp.bfloat16

D_MODEL = 1024
N_MEM = 256
SB_HEADS = 8
DSA_HEADS = 8
DSA_KV_HEADS = 2
IDX_HEADS = 8
HEAD_DIM = 64
TOPK_MAX = 256
MEM_HEADS = 4
MEM_HEAD_DIM = 128
D_FF = 4 * D_MODEL
ROPE_THETA = 10000.0
EPS = 1e-6

LANES = 128
HALF_ROT = HEAD_DIM // 2

W_SB = SB_HEADS * HEAD_DIM
OFF_SBQ = 0
OFF_SBK = OFF_SBQ + W_SB
OFF_SBV = OFF_SBK + W_SB
OFF_DQ = OFF_SBV + W_SB
OFF_DK = OFF_DQ + DSA_HEADS * HEAD_DIM
OFF_DV = OFF_DK + DSA_KV_HEADS * HEAD_DIM
OFF_IQ = OFF_DV + DSA_KV_HEADS * HEAD_DIM
OFF_IK = OFF_IQ + IDX_HEADS * HEAD_DIM
OFF_IW = OFF_IK + LANES
D_IN_PACKED = OFF_IW + LANES

ROW_TILE = 512
SB_TILE = 256
DSA_TQ = 128
DSA_KCHUNK = 512
VMEM_LIMIT = 56 * 1024 * 1024
NEG_BIG = -1e30
INT_MIN = -(2 ** 31)


def _rms(x, g):
    return x * lax.rsqrt(jnp.mean(x * x, axis=-1, keepdims=True) + EPS) * g


def _dot(a, b):
    return jnp.dot(a, b, preferred_element_type=F32)


def _dot_t(a, b):
    return lax.dot_general(a, b, (((1,), (1,)), ((), ())), preferred_element_type=F32)


def _proj_kernel(x_ref, g_ref, w_ref, cos_ref, sin_ref,
                 sbq_ref, sbk_ref, sbv_ref, dq_ref, dk_ref, dv_ref, iq_ref, ik_ref, iw_ref):
    ub = _rms(x_ref[...], g_ref[...]).astype(BF16)
    cos = cos_ref[...]
    sin = sin_ref[...]
    lane = lax.broadcasted_iota(jnp.int32, (1, LANES), 1)
    first_half = (lane % HEAD_DIM) < HALF_ROT

    def mm(off, width=LANES):
        return _dot(ub, w_ref[:, off:off + width])

    def rope(s):
        rot = jnp.where(first_half, pltpu.roll(s, LANES - HALF_ROT, 1), pltpu.roll(s, HALF_ROT, 1))
        return s * cos + rot * sin

    q_scale = HEAD_DIM ** -0.5
    sbq_ref[...] = (mm(OFF_SBQ, W_SB) * q_scale).astype(BF16)
    sbk_ref[...] = mm(OFF_SBK, W_SB).astype(BF16)
    sbv_ref[...] = mm(OFF_SBV, W_SB).astype(BF16)
    for p in range(4):
        sl = slice(p * LANES, (p + 1) * LANES)
        dq_ref[:, sl] = (rope(mm(OFF_DQ + p * LANES)) * q_scale).astype(BF16)
        iq_ref[:, sl] = (rope(mm(OFF_IQ + p * LANES)) * q_scale).astype(BF16)
    dk_ref[...] = rope(mm(OFF_DK)).astype(BF16)
    dv_ref[...] = mm(OFF_DV).astype(BF16)
    ik_ref[...] = rope(mm(OFF_IK)).astype(BF16)
    iw_ref[...] = mm(OFF_IW) * (IDX_HEADS ** -0.5)


def _proj_call(x2, g, w_packed, cos_t, sin_t, seq):
    n = x2.shape[0]
    tm = ROW_TILE
    tiles_per_seq = seq // tm
    row = lambda i: (i, 0)
    const = lambda i: (0, 0)
    pos = lambda i: (i % tiles_per_seq, 0)
    out_w = [W_SB, W_SB, W_SB, 512, LANES, LANES, 512, LANES, LANES]
    out_dt = [BF16] * 8 + [F32]
    return pl.pallas_call(
        _proj_kernel,
        grid=(n // tm,),
        in_specs=[
            pl.BlockSpec((tm, D_MODEL), row),
            pl.BlockSpec((1, D_MODEL), const),
            pl.BlockSpec((D_MODEL, D_IN_PACKED), const),
            pl.BlockSpec((tm, LANES), pos),
            pl.BlockSpec((tm, LANES), pos),
        ],
        out_specs=[pl.BlockSpec((tm, w), row) for w in out_w],
        out_shape=[jax.ShapeDtypeStruct((n, w), dt) for w, dt in zip(out_w, out_dt)],
        compiler_params=pltpu.CompilerParams(
            dimension_semantics=("arbitrary",), vmem_limit_bytes=VMEM_LIMIT),
        name="proj",
    )(x2, g, w_packed, cos_t, sin_t)


def _sb_kernel(q_ref, k_ref, v_ref, o_ref):
    t = SB_TILE
    i = pl.program_id(2)
    q = q_ref[...]
    lane = lax.broadcasted_iota(jnp.int32, (1, LANES), 1)
    row = lax.broadcasted_iota(jnp.int32, (t, t), 0)
    col = lax.broadcasted_iota(jnp.int32, (t, t), 1)
    strict = col < row
    later = jnp.where(row > col, 1.0, 0.0).astype(BF16)
    zero_q = jnp.zeros_like(q)
    qh = [jnp.where((lane // HEAD_DIM) == h, q, zero_q) for h in range(2)]

    def tile(start, diag, qm, carry, acc):
        kb = k_ref[pl.ds(start, t), :]
        vb = v_ref[pl.ds(start, t), :]
        z = _dot_t(qm, kb)
        sp = jnp.log(1.0 + jnp.exp(-jnp.abs(z)))
        log_beta = jnp.minimum(z, 0.0) - sp
        log_1mb = jnp.minimum(-z, 0.0) - sp
        if diag:
            log_1mb = jnp.where(strict, log_1mb, 0.0)
        suffix = _dot(log_1mb.astype(BF16), later) + carry
        a = jnp.exp(log_beta + suffix)
        if diag:
            a = jnp.where(strict, a, 0.0)
        acc = acc + _dot(a.astype(BF16), vb)
        carry = carry + jnp.sum(log_1mb, axis=1, keepdims=True)
        return carry, acc

    zc = jnp.zeros((t, 1), F32)
    za = jnp.zeros((t, LANES), F32)
    diag_start = pl.multiple_of(i * t, t)
    state = []
    for h in range(2):
        state.extend(tile(diag_start, True, qh[h], zc, za))

    def body(s, st):
        start = pl.multiple_of((i - 1 - s) * t, t)
        c0, a0 = tile(start, False, qh[0], st[0], st[1])
        c1, a1 = tile(start, False, qh[1], st[2], st[3])
        return (c0, a0, c1, a1)

    st = lax.fori_loop(0, i, body, tuple(state))
    o_ref[...] = jnp.where(lane < HEAD_DIM, st[1], st[3]).astype(BF16)


def _sb_call(q, k, v, batch, seq):
    t = SB_TILE
    nq = seq // t
    pairs = W_SB // LANES
    qmap = lambda b, p, i: (b * nq + i, p)
    kvmap = lambda b, p, i: (b, p)
    return pl.pallas_call(
        _sb_kernel,
        grid=(batch, pairs, nq),
        in_specs=[
            pl.BlockSpec((t, LANES), qmap),
            pl.BlockSpec((seq, LANES), kvmap),
            pl.BlockSpec((seq, LANES), kvmap),
        ],
        out_specs=pl.BlockSpec((t, LANES), qmap),
        out_shape=jax.ShapeDtypeStruct((batch * seq, W_SB), BF16),
        compiler_params=pltpu.CompilerParams(
            dimension_semantics=("arbitrary", "arbitrary", "arbitrary"),
            vmem_limit_bytes=VMEM_LIMIT),
        name="stickbreak",
    )(q, k, v)


def _dsa_tile(width, i, dq_ref, iq_ref, iw_ref, dk_ref, dv_ref, ik_ref, o_ref, key_sc, bias_sc):
    tq = DSA_TQ
    lane = lax.broadcasted_iota(jnp.int32, (1, LANES), 1)
    half = lane // HEAD_DIM
    t_idx = i * tq + lax.broadcasted_iota(jnp.int32, (tq, 1), 0)

    iq = iq_ref[...]
    zero_q = jnp.zeros((tq, LANES), BF16)
    qstack = jnp.concatenate(
        [jnp.where(half == (h % 2), iq[:, (h // 2) * LANES:(h // 2 + 1) * LANES], zero_q)
         for h in range(IDX_HEADS)], axis=0)
    iw = iw_ref[...]
    wcol = [iw[:, h:h + 1] for h in range(IDX_HEADS)]
    for c in range(width // DSA_KCHUNK):
        ksl = slice(c * DSA_KCHUNK, (c + 1) * DSA_KCHUNK)
        dots = _dot_t(qstack, ik_ref[ksl, :])
        score = jnp.zeros((tq, DSA_KCHUNK), F32)
        for h in range(IDX_HEADS):
            score = score + jnp.maximum(dots[h * tq:(h + 1) * tq], 0.0) * wcol[h]
        kpos = c * DSA_KCHUNK + lax.broadcasted_iota(jnp.int32, (1, DSA_KCHUNK), 1)
        score = jnp.where(kpos <= t_idx, score, -jnp.inf)
        bits = lax.bitcast_convert_type(score, jnp.int32)
        key_sc[:, ksl] = bits ^ ((bits >> 31) & 0x7FFFFFFF)

    k_row = jnp.minimum(t_idx + 1, TOPK_MAX).astype(F32)

    def count(pred):
        return jnp.sum(jnp.where(pred, 1.0, 0.0), axis=1, keepdims=True)

    def bisect(it, thr):
        cand = thr + jnp.left_shift(jnp.int32(1), 31 - it)
        return jnp.where(count(key_sc[:, :width] >= cand) >= k_row, cand, thr)

    thr = lax.fori_loop(0, 32, bisect, jnp.full((tq, 1), INT_MIN, jnp.int32), unroll=4)

    n_ge = count(key_sc[:, :width] >= thr)
    surplus = jnp.max(n_ge - k_row)

    @pl.when(surplus <= 0.0)
    def _():
        bias_sc[:, :width] = jnp.where(key_sc[:, :width] >= thr, 0.0, NEG_BIG)

    @pl.when(surplus > 0.0)
    def _():
        need = k_row - count(key_sc[:, :width] > thr)
        r128 = lax.broadcasted_iota(jnp.int32, (LANES, LANES), 0)
        c128 = lax.broadcasted_iota(jnp.int32, (LANES, LANES), 1)
        earlier = jnp.where(r128 < c128, 1.0, 0.0).astype(BF16)
        ties_before = jnp.zeros((tq, 1), F32)
        for c in range(width // LANES):
            ksl = slice(c * LANES, (c + 1) * LANES)
            kc = key_sc[:, ksl]
            eq = kc == thr
            eqf = jnp.where(eq, 1.0, 0.0)
            rank = _dot(eqf.astype(BF16), earlier) + ties_before
            tie_bias = jnp.where(rank < need, 0.0, NEG_BIG)
            bias_sc[:, ksl] = jnp.where(kc > thr, 0.0, jnp.where(eq, tie_bias, NEG_BIG))
            ties_before = ties_before + jnp.sum(eqf, axis=1, keepdims=True)

    dq = dq_ref[...]
    dk = dk_ref[:width, :]
    dv = dv_ref[:width, :]
    for p in range(DSA_HEADS // 2):
        slab = dq[:, p * LANES:(p + 1) * LANES]
        outs = []
        for g in range(DSA_KV_HEADS):
            qh = jnp.where(half == g, slab, zero_q)
            logits = _dot_t(qh, dk) + bias_sc[:, :width]
            m = jnp.max(logits, axis=1, keepdims=True)
            pexp = jnp.exp(logits - m)
            denom = jnp.sum(pexp, axis=1, keepdims=True)
            outs.append(_dot(pexp.astype(BF16), dv) / denom)
        o_ref[:, p * LANES:(p + 1) * LANES] = jnp.where(half == 0, outs[0], outs[1]).astype(BF16)


def _dsa_kernel(dq_ref, iq_ref, iw_ref, dk_ref, dv_ref, ik_ref, o_ref, key_sc, bias_sc, *, seq):
    i = pl.program_id(1)
    tiles_per_width = DSA_KCHUNK // DSA_TQ
    for v in range(seq // DSA_KCHUNK):
        @pl.when(i // tiles_per_width == v)
        def _(v=v):
            _dsa_tile((v + 1) * DSA_KCHUNK, i, dq_ref, iq_ref, iw_ref, dk_ref, dv_ref, ik_ref,
                      o_ref, key_sc, bias_sc)


def _dsa_call(dq, iq, iw, dk, dv, ik, batch, seq):
    tq = DSA_TQ
    nq = seq // tq
    qmap = lambda b, i: (b * nq + i, 0)
    kvmap = lambda b, i: (b, 0)
    return pl.pallas_call(
        functools.partial(_dsa_kernel, seq=seq),
        grid=(batch, nq),
        in_specs=[
            pl.BlockSpec((tq, 512), qmap),
            pl.BlockSpec((tq, 512), qmap),
            pl.BlockSpec((tq, LANES), qmap),
            pl.BlockSpec((seq, LANES), kvmap),
            pl.BlockSpec((seq, LANES), kvmap),
            pl.BlockSpec((seq, LANES), kvmap),
        ],
        out_specs=pl.BlockSpec((tq, 512), qmap),
        out_shape=jax.ShapeDtypeStruct((batch * seq, 512), BF16),
        scratch_shapes=[pltpu.VMEM((tq, seq), jnp.int32), pltpu.VMEM((tq, seq), F32)],
        compiler_params=pltpu.CompilerParams(
            dimension_semantics=("arbitrary", "arbitrary"), vmem_limit_bytes=VMEM_LIMIT),
        name="dsa",
    )(dq, iq, iw, dk, dv, ik)


def _memkv_kernel(m_ref, g_ref, w_ref, k_ref, v_ref):
    mb = _rms(m_ref[...], g_ref[...]).astype(BF16)
    kv = _dot(mb, w_ref[...])
    half = MEM_HEADS * MEM_HEAD_DIM
    k_ref[...] = kv[:, :half].astype(BF16)
    v_ref[...] = kv[:, half:].astype(BF16)


def _memkv_call(mem2, g, w_ckv, batch):
    half = MEM_HEADS * MEM_HEAD_DIM
    row = lambda b: (b, 0)
    const = lambda b: (0, 0)
    return pl.pallas_call(
        _memkv_kernel,
        grid=(batch,),
        in_specs=[
            pl.BlockSpec((N_MEM, D_MODEL), row),
            pl.BlockSpec((1, D_MODEL), const),
            pl.BlockSpec((D_MODEL, 2 * half), const),
        ],
        out_specs=[pl.BlockSpec((N_MEM, half), row)] * 2,
        out_shape=[jax.ShapeDtypeStruct((batch * N_MEM, half), BF16)] * 2,
        compiler_params=pltpu.CompilerParams(
            dimension_semantics=("arbitrary",), vmem_limit_bytes=VMEM_LIMIT),
        name="memkv",
    )(mem2, g, w_ckv)


def _mix_kernel(x_ref, osb_ref, odsa_ref, g1_ref, wg_ref, bg_ref, wbs_ref, wbd_ref, wo_ref,
                g2_ref, wcq_ref, km_ref, vm_ref, wco_ref, h_ref):
    x = x_ref[...]
    ub = _rms(x, g1_ref[...]).astype(BF16)
    gates = 1.0 / (1.0 + jnp.exp(-(_dot(ub, wg_ref[...]) + bg_ref[...])))
    merged = (gates[:, :D_MODEL] * _dot(osb_ref[...], wbs_ref[...])
              + gates[:, D_MODEL:] * _dot(odsa_ref[...], wbd_ref[...]))
    h1 = x + _dot(merged.astype(BF16), wo_ref[...])

    u2 = _rms(h1, g2_ref[...]).astype(BF16)
    qb = _dot(u2, wcq_ref[...]).astype(BF16)
    km = km_ref[...]
    vm = vm_ref[...]
    outs = []
    for h in range(MEM_HEADS):
        sl = slice(h * MEM_HEAD_DIM, (h + 1) * MEM_HEAD_DIM)
        logits = _dot_t(qb[:, sl], km[:, sl]) * (MEM_HEAD_DIM ** -0.5)
        m = jnp.max(logits, axis=1, keepdims=True)
        pexp = jnp.exp(logits - m)
        denom = jnp.sum(pexp, axis=1, keepdims=True)
        outs.append((_dot(pexp.astype(BF16), vm[:, sl]) / denom).astype(BF16))
    o = jnp.concatenate(outs, axis=1)
    h_ref[...] = h1 + _dot(o, wco_ref[...])


def _mix_call(x2, osb, odsa, g1, wg, bg, wbs, wbd, wo, g2, wcq, km, vm, wco, seq):
    n = x2.shape[0]
    tm = ROW_TILE
    tiles_per_seq = seq // tm
    half = MEM_HEADS * MEM_HEAD_DIM
    row = lambda i: (i, 0)
    const = lambda i: (0, 0)
    bat = lambda i: (i // tiles_per_seq, 0)

    def full(a):
        return pl.BlockSpec(a.shape, const)

    return pl.pallas_call(
        _mix_kernel,
        grid=(n // tm,),
        in_specs=[
            pl.BlockSpec((tm, D_MODEL), row),
            pl.BlockSpec((tm, 512), row),
            pl.BlockSpec((tm, 512), row),
            full(g1), full(wg), full(bg), full(wbs), full(wbd), full(wo),
            full(g2), full(wcq),
            pl.BlockSpec((N_MEM, half), bat),
            pl.BlockSpec((N_MEM, half), bat),
            full(wco),
        ],
        out_specs=pl.BlockSpec((tm, D_MODEL), row),
        out_shape=jax.ShapeDtypeStruct((n, D_MODEL), F32),
        compiler_params=pltpu.CompilerParams(
            dimension_semantics=("arbitrary",), vmem_limit_bytes=VMEM_LIMIT),
        name="mix",
    )(x2, osb, odsa, g1, wg, bg, wbs, wbd, wo, g2, wcq, km, vm, wco)


def _mlp_kernel(h_ref, g_ref, wu_ref, wd_ref, gf_ref, o_ref, *, final_norm):
    h = h_ref[...]
    ub = _rms(h, g_ref[...]).astype(BF16)
    acc = h
    for c in range(D_FF // D_MODEL):
        sl = slice(c * D_MODEL, (c + 1) * D_MODEL)
        hid = jnp.maximum(_dot(ub, wu_ref[:, sl]), 0.0)
        acc = acc + _dot((hid * hid).astype(BF16), wd_ref[sl, :])
    if final_norm:
        acc = _rms(acc, gf_ref[...])
    o_ref[...] = acc


def _mlp_call(h2, g, wu, wd, gf, final_norm):
    n = h2.shape[0]
    tm = ROW_TILE
    row = lambda i: (i, 0)
    const = lambda i: (0, 0)
    return pl.pallas_call(
        functools.partial(_mlp_kernel, final_norm=final_norm),
        grid=(n // tm,),
        in_specs=[
            pl.BlockSpec((tm, D_MODEL), row),
            pl.BlockSpec((1, D_MODEL), const),
            pl.BlockSpec((D_MODEL, D_FF), const),
            pl.BlockSpec((D_FF, D_MODEL), const),
            pl.BlockSpec((1, D_MODEL), const),
        ],
        out_specs=pl.BlockSpec((tm, D_MODEL), row),
        out_shape=jax.ShapeDtypeStruct((n, D_MODEL), F32),
        compiler_params=pltpu.CompilerParams(
            dimension_semantics=("arbitrary",), vmem_limit_bytes=VMEM_LIMIT),
        name="mlp",
    )(h2, g, wu, wd, gf)


def _pack_w_in(w_in):
    offs = np.cumsum([0, 512, 512, 512, 512, 128, 128, 512, 64, 8])
    sbq, sbk, sbv, dq, dk, dv, iq, ik, iw = [w_in[:, offs[j]:offs[j + 1]] for j in range(9)]
    dq = dq.reshape(D_MODEL, 2, 4, HEAD_DIM).transpose(0, 2, 1, 3).reshape(D_MODEL, 512)
    iw = jnp.pad(iw, ((0, 0), (0, LANES - IDX_HEADS)))
    return jnp.concatenate([sbq, sbk, sbv, dq, dk, dv, iq, ik, ik, iw], axis=1).astype(BF16)


def _rope_tables(seq):
    inv = ROPE_THETA ** (-jnp.arange(0, HEAD_DIM, 2, dtype=F32) / HEAD_DIM)
    ang = jnp.arange(seq).astype(F32)[:, None] * inv[None, :]
    cos = jnp.cos(ang)
    sin = jnp.sin(ang)
    cos_t = jnp.tile(cos, (1, LANES // HALF_ROT))
    sin_t = jnp.tile(jnp.concatenate([-sin, sin], axis=1), (1, LANES // HEAD_DIM))
    return cos_t, sin_t


def kernel(x, mem, norm_mix, w_in, w_branch_sb, w_branch_dsa, w_gate, b_gate, w_out, norm_cross,
           norm_mem, w_cq, w_ckv, w_co, norm_mlp, w_up, w_down, norm_final):
    batch, seq, d = x.shape
    depth = w_in.shape[0]
    cos_t, sin_t = _rope_tables(seq)
    h = x.reshape(batch * seq, d)
    mem2 = mem.reshape(batch * mem.shape[1], d)
    gf = norm_final.reshape(1, d)
    for l in range(depth):
        w_packed = _pack_w_in(w_in[l])
        wbd = (w_branch_dsa[l].reshape(2, 4, HEAD_DIM, d).transpose(1, 0, 2, 3)
               .reshape(DSA_HEADS * HEAD_DIM, d).astype(BF16))
        sbq, sbk, sbv, dq, dk, dv, iq, ik, iw = _proj_call(
            h, norm_mix[l].reshape(1, d), w_packed, cos_t, sin_t, seq)
        o_sb = _sb_call(sbq, sbk, sbv, batch, seq)
        o_dsa = _dsa_call(dq, iq, iw, dk, dv, ik, batch, seq)
        km, vm = _memkv_call(mem2, norm_mem[l].reshape(1, d), w_ckv[l].astype(BF16), batch)
        h2 = _mix_call(h, o_sb, o_dsa, norm_mix[l].reshape(1, d), w_gate[l].astype(BF16),
                       b_gate[l].reshape(1, -1), w_branch_sb[l].astype(BF16), wbd,
                       w_out[l].astype(BF16), norm_cross[l].reshape(1, d), w_cq[l].astype(BF16),
                       km, vm, w_co[l].astype(BF16), seq)
        h = _mlp_call(h2, norm_mlp[l].reshape(1, d), w_up[l].astype(BF16), w_down[l].astype(BF16),
                      gf, final_norm=(l == depth - 1))
    return h.reshape(batch, seq, d)
```

```python
import functools

import jax
import jax.numpy as jnp
import numpy as np
from jax import lax
from jax.experimental import pallas as pl
from jax.experimental.pallas import tpu as pltpu

F32 = jnp.float32
BF16 = jnp.bfloat16

D_MODEL = 1024
N_MEM = 256
SB_HEADS = 8
DSA_HEADS = 8
DSA_KV_HEADS = 2
IDX_HEADS = 8
HEAD_DIM = 64
TOPK_MAX = 256
MEM_HEADS = 4
MEM_HEAD_DIM = 128
D_FF = 4 * D_MODEL
ROPE_THETA = 10000.0
EPS = 1e-6

LANES = 128
HALF_ROT = HEAD_DIM // 2

W_SB = SB_HEADS * HEAD_DIM
OFF_SBQ = 0
OFF_SBK = OFF_SBQ + W_SB
OFF_SBV = OFF_SBK + W_SB
OFF_DQ = OFF_SBV + W_SB
OFF_DK = OFF_DQ + DSA_HEADS * HEAD_DIM
OFF_DV = OFF_DK + DSA_KV_HEADS * HEAD_DIM
OFF_IQ = OFF_DV + DSA_KV_HEADS * HEAD_DIM
OFF_IK = OFF_IQ + IDX_HEADS * HEAD_DIM
OFF_IW = OFF_IK + LANES
D_IN_PACKED = OFF_IW + LANES

ROW_TILE = 512
SB_TILE = 256
DSA_TQ = 128
DSA_KCHUNK = 512
VMEM_LIMIT = 56 * 1024 * 1024
NEG_BIG = -1e30
INT_MIN = -(2 ** 31)
LOG2_E = 1.4426950408889634


def _rms(x, g):
    return x * lax.rsqrt(jnp.mean(x * x, axis=-1, keepdims=True) + EPS) * g


def _dot(a, b):
    return jnp.dot(a, b, preferred_element_type=F32)


def _dot_t(a, b):
    return lax.dot_general(a, b, (((1,), (1,)), ((), ())), preferred_element_type=F32)


def _proj_kernel(x_ref, g_ref, w_ref, cos_ref, sin_ref,
                 sbq_ref, sbk_ref, sbv_ref, dq_ref, dk_ref, dv_ref, iq_ref, ik_ref, iw_ref):
    ub = _rms(x_ref[...], g_ref[...]).astype(BF16)
    cos = cos_ref[...]
    sin = sin_ref[...]
    lane = lax.broadcasted_iota(jnp.int32, (1, LANES), 1)
    first_half = (lane % HEAD_DIM) < HALF_ROT

    def mm(off, width=LANES):
        return _dot(ub, w_ref[:, off:off + width])

    def rope(s):
        rot = jnp.where(first_half, pltpu.roll(s, LANES - HALF_ROT, 1), pltpu.roll(s, HALF_ROT, 1))
        return s * cos + rot * sin

    q_scale = HEAD_DIM ** -0.5
    sbq_ref[...] = (mm(OFF_SBQ, W_SB) * (q_scale * LOG2_E)).astype(BF16)
    sbk_ref[...] = mm(OFF_SBK, W_SB).astype(BF16)
    sbv_ref[...] = mm(OFF_SBV, W_SB).astype(BF16)
    for p in range(4):
        sl = slice(p * LANES, (p + 1) * LANES)
        dq_ref[:, sl] = (rope(mm(OFF_DQ + p * LANES)) * q_scale).astype(BF16)
        iq_ref[:, sl] = (rope(mm(OFF_IQ + p * LANES)) * q_scale).astype(BF16)
    dk_ref[...] = rope(mm(OFF_DK)).astype(BF16)
    dv_ref[...] = mm(OFF_DV).astype(BF16)
    ik_ref[...] = rope(mm(OFF_IK)).astype(BF16)
    iw_ref[...] = mm(OFF_IW) * (IDX_HEADS ** -0.5)


def _proj_call(x2, g, w_packed, cos_t, sin_t, seq):
    n = x2.shape[0]
    tm = ROW_TILE
    tiles_per_seq = seq // tm
    row = lambda i: (i, 0)
    const = lambda i: (0, 0)
    pos = lambda i: (i % tiles_per_seq, 0)
    out_w = [W_SB, W_SB, W_SB, 512, LANES, LANES, 512, LANES, LANES]
    out_dt = [BF16] * 8 + [F32]
    return pl.pallas_call(
        _proj_kernel,
        grid=(n // tm,),
        in_specs=[
            pl.BlockSpec((tm, D_MODEL), row),
            pl.BlockSpec((1, D_MODEL), const),
            pl.BlockSpec((D_MODEL, D_IN_PACKED), const),
            pl.BlockSpec((tm, LANES), pos),
            pl.BlockSpec((tm, LANES), pos),
        ],
        out_specs=[pl.BlockSpec((tm, w), row) for w in out_w],
        out_shape=[jax.ShapeDtypeStruct((n, w), dt) for w, dt in zip(out_w, out_dt)],
        compiler_params=pltpu.CompilerParams(
            dimension_semantics=("arbitrary",), vmem_limit_bytes=VMEM_LIMIT),
        name="proj",
    )(x2, g, w_packed, cos_t, sin_t)


def _sb_kernel(q_ref, k_ref, v_ref, o_ref, acc_sc):
    t = SB_TILE
    i = pl.program_id(1)
    lane = lax.broadcasted_iota(jnp.int32, (1, LANES), 1)
    first_head = lane < HEAD_DIM
    row = lax.broadcasted_iota(jnp.int32, (t, t), 0)
    col = lax.broadcasted_iota(jnp.int32, (t, t), 1)
    strict = col < row
    later = jnp.where(row > col, 1.0, 0.0).astype(BF16)
    zero_q = jnp.zeros((t, LANES), BF16)
    qh = []
    for p in range(SB_HEADS // 2):
        slab = q_ref[:, p * LANES:(p + 1) * LANES]
        qh.append(jnp.where(first_head, slab, zero_q))
        qh.append(jnp.where(first_head, zero_q, slab))

    def tile(start, diag, carries):
        new = []
        for p in range(SB_HEADS // 2):
            sl = slice(p * LANES, (p + 1) * LANES)
            kb = k_ref[pl.ds(start, t), sl]
            vb = v_ref[pl.ds(start, t), sl]
            av = []
            for h in range(2):
                z = _dot_t(qh[2 * p + h], kb)
                sp = jnp.log2(1.0 + jnp.exp2(-jnp.abs(z)))
                log_beta = jnp.minimum(z, 0.0) - sp
                neg_log_1mb = jnp.maximum(z, 0.0) + sp
                if diag:
                    neg_log_1mb = jnp.where(strict, neg_log_1mb, 0.0)
                suffix = _dot(neg_log_1mb.astype(BF16), later) + carries[2 * p + h]
                a = jnp.exp2(log_beta - suffix)
                if diag:
                    a = jnp.where(strict, a, 0.0)
                av.append(_dot(a.astype(BF16), vb))
                new.append(carries[2 * p + h] + jnp.sum(neg_log_1mb, axis=1, keepdims=True))
            contrib = jnp.where(first_head, av[0], av[1])
            if diag:
                acc_sc[:, sl] = contrib
            else:
                acc_sc[:, sl] += contrib
        return tuple(new)

    zc = jnp.zeros((t, 1), F32)
    carries = tile(pl.multiple_of(i * t, t), True, (zc,) * SB_HEADS)

    def body(s, carries):
        return tile(pl.multiple_of((i - 1 - s) * t, t), False, carries)

    lax.fori_loop(0, i, body, carries)
    o_ref[...] = acc_sc[...].astype(BF16)


def _sb_call(q, k, v, batch, seq):
    t = SB_TILE
    nq = seq // t
    qmap = lambda b, i: (b * nq + i, 0)
    kvmap = lambda b, i: (b, 0)
    return pl.pallas_call(
        _sb_kernel,
        grid=(batch, nq),
        in_specs=[
            pl.BlockSpec((t, W_SB), qmap),
            pl.BlockSpec((seq, W_SB), kvmap),
            pl.BlockSpec((seq, W_SB), kvmap),
        ],
        out_specs=pl.BlockSpec((t, W_SB), qmap),
        out_shape=jax.ShapeDtypeStruct((batch * seq, W_SB), BF16),
        scratch_shapes=[pltpu.VMEM((t, W_SB), F32)],
        compiler_params=pltpu.CompilerParams(
            dimension_semantics=("arbitrary", "arbitrary"), vmem_limit_bytes=VMEM_LIMIT),
        name="stickbreak",
    )(q, k, v)


def _dsa_tile(width, i, dq_ref, iq_ref, iw_ref, dk_ref, dv_ref, ik_ref, o_ref, key_sc, bias_sc):
    tq = DSA_TQ
    lane = lax.broadcasted_iota(jnp.int32, (1, LANES), 1)
    half = lane // HEAD_DIM
    t_idx = i * tq + lax.broadcasted_iota(jnp.int32, (tq, 1), 0)

    iq = iq_ref[...]
    zero_q = jnp.zeros((tq, LANES), BF16)
    qstack = jnp.concatenate(
        [jnp.where(half == (h % 2), iq[:, (h // 2) * LANES:(h // 2 + 1) * LANES], zero_q)
         for h in range(IDX_HEADS)], axis=0)
    iw = iw_ref[...]
    wcol = [iw[:, h:h + 1] for h in range(IDX_HEADS)]
    for c in range(width // DSA_KCHUNK):
        ksl = slice(c * DSA_KCHUNK, (c + 1) * DSA_KCHUNK)
        dots = _dot_t(qstack, ik_ref[ksl, :])
        score = jnp.zeros((tq, DSA_KCHUNK), F32)
        for h in range(IDX_HEADS):
            score = score + jnp.maximum(dots[h * tq:(h + 1) * tq], 0.0) * wcol[h]
        kpos = c * DSA_KCHUNK + lax.broadcasted_iota(jnp.int32, (1, DSA_KCHUNK), 1)
        score = jnp.where(kpos <= t_idx, score, -jnp.inf)
        bits = lax.bitcast_convert_type(score, jnp.int32)
        key_sc[:, ksl] = bits ^ ((bits >> 31) & 0x7FFFFFFF)

    k_row = jnp.minimum(t_idx + 1, TOPK_MAX).astype(F32)

    def count(pred):
        return jnp.sum(jnp.where(pred, 1.0, 0.0), axis=1, keepdims=True)

    def bisect(it, thr):
        cand = thr + jnp.left_shift(jnp.int32(1), 31 - it)
        return jnp.where(count(key_sc[:, :width] >= cand) >= k_row, cand, thr)

    thr = lax.fori_loop(0, 32, bisect, jnp.full((tq, 1), INT_MIN, jnp.int32), unroll=4)

    n_ge = count(key_sc[:, :width] >= thr)
    surplus = jnp.max(n_ge - k_row)

    @pl.when(surplus <= 0.0)
    def _():
        bias_sc[:, :width] = jnp.where(key_sc[:, :width] >= thr, 0.0, NEG_BIG)

    @pl.when(surplus > 0.0)
    def _():
        need = k_row - count(key_sc[:, :width] > thr)
        r128 = lax.broadcasted_iota(jnp.int32, (LANES, LANES), 0)
        c128 = lax.broadcasted_iota(jnp.int32, (LANES, LANES), 1)
        earlier = jnp.where(r128 < c128, 1.0, 0.0).astype(BF16)
        ties_before = jnp.zeros((tq, 1), F32)
        for c in range(width // LANES):
            ksl = slice(c * LANES, (c + 1) * LANES)
            kc = key_sc[:, ksl]
            eq = kc == thr
            eqf = jnp.where(eq, 1.0, 0.0)
            rank = _dot(eqf.astype(BF16), earlier) + ties_before
            tie_bias = jnp.where(rank < need, 0.0, NEG_BIG)
            bias_sc[:, ksl] = jnp.where(kc > thr, 0.0, jnp.where(eq, tie_bias, NEG_BIG))
            ties_before = ties_before + jnp.sum(eqf, axis=1, keepdims=True)

    dq = dq_ref[...]
    dk = dk_ref[:width, :]
    dv = dv_ref[:width, :]
    for p in range(DSA_HEADS // 2):
        slab = dq[:, p * LANES:(p + 1) * LANES]
        outs = []
        for g in range(DSA_KV_HEADS):
            qh = jnp.where(half == g, slab, zero_q)
            logits = _dot_t(qh, dk) + bias_sc[:, :width]
            m = jnp.max(logits, axis=1, keepdims=True)
            pexp = jnp.exp(logits - m)
            denom = jnp.sum(pexp, axis=1, keepdims=True)
            outs.append(_dot(pexp.astype(BF16), dv) / denom)
        o_ref[:, p * LANES:(p + 1) * LANES] = jnp.where(half == 0, outs[0], outs[1]).astype(BF16)


def _dsa_kernel(dq_ref, iq_ref, iw_ref, dk_ref, dv_ref, ik_ref, o_ref, key_sc, bias_sc, *, seq):
    i = pl.program_id(1)
    tiles_per_width = DSA_KCHUNK // DSA_TQ
    for v in range(seq // DSA_KCHUNK):
        @pl.when(i // tiles_per_width == v)
        def _(v=v):
            _dsa_tile((v + 1) * DSA_KCHUNK, i, dq_ref, iq_ref, iw_ref, dk_ref, dv_ref, ik_ref,
                      o_ref, key_sc, bias_sc)


def _dsa_call(dq, iq, iw, dk, dv, ik, batch, seq):
    tq = DSA_TQ
    nq = seq // tq
    qmap = lambda b, i: (b * nq + i, 0)
    kvmap = lambda b, i: (b, 0)
    return pl.pallas_call(
        functools.partial(_dsa_kernel, seq=seq),
        grid=(batch, nq),
        in_specs=[
            pl.BlockSpec((tq, 512), qmap),
            pl.BlockSpec((tq, 512), qmap),
            pl.BlockSpec((tq, LANES), qmap),
            pl.BlockSpec((seq, LANES), kvmap),
            pl.BlockSpec((seq, LANES), kvmap),
            pl.BlockSpec((seq, LANES), kvmap),
        ],
        out_specs=pl.BlockSpec((tq, 512), qmap),
        out_shape=jax.ShapeDtypeStruct((batch * seq, 512), BF16),
        scratch_shapes=[pltpu.VMEM((tq, seq), jnp.int32), pltpu.VMEM((tq, seq), F32)],
        compiler_params=pltpu.CompilerParams(
            dimension_semantics=("arbitrary", "arbitrary"), vmem_limit_bytes=VMEM_LIMIT),
        name="dsa",
    )(dq, iq, iw, dk, dv, ik)


def _memkv_kernel(m_ref, g_ref, w_ref, k_ref, v_ref):
    mb = _rms(m_ref[...], g_ref[...]).astype(BF16)
    kv = _dot(mb, w_ref[...])
    half = MEM_HEADS * MEM_HEAD_DIM
    k_ref[...] = kv[:, :half].astype(BF16)
    v_ref[...] = kv[:, half:].astype(BF16)


def _memkv_call(mem2, g, w_ckv, batch):
    half = MEM_HEADS * MEM_HEAD_DIM
    row = lambda b: (b, 0)
    const = lambda b: (0, 0)
    return pl.pallas_call(
        _memkv_kernel,
        grid=(batch,),
        in_specs=[
            pl.BlockSpec((N_MEM, D_MODEL), row),
            pl.BlockSpec((1, D_MODEL), const),
            pl.BlockSpec((D_MODEL, 2 * half), const),
        ],
        out_specs=[pl.BlockSpec((N_MEM, half), row)] * 2,
        out_shape=[jax.ShapeDtypeStruct((batch * N_MEM, half), BF16)] * 2,
        compiler_params=pltpu.CompilerParams(
            dimension_semantics=("arbitrary",), vmem_limit_bytes=VMEM_LIMIT),
        name="memkv",
    )(mem2, g, w_ckv)


def _mix_kernel(x_ref, osb_ref, odsa_ref, g1_ref, wg_ref, bg_ref, wbs_ref, wbd_ref, wo_ref,
                g2_ref, wcq_ref, km_ref, vm_ref, wco_ref, h_ref):
    x = x_ref[...]
    ub = _rms(x, g1_ref[...]).astype(BF16)
    gates = 1.0 / (1.0 + jnp.exp(-(_dot(ub, wg_ref[...]) + bg_ref[...])))
    merged = (gates[:, :D_MODEL] * _dot(osb_ref[...], wbs_ref[...])
              + gates[:, D_MODEL:] * _dot(odsa_ref[...], wbd_ref[...]))
    h1 = x + _dot(merged.astype(BF16), wo_ref[...])

    u2 = _rms(h1, g2_ref[...]).astype(BF16)
    qb = _dot(u2, wcq_ref[...]).astype(BF16)
    km = km_ref[...]
    vm = vm_ref[...]
    outs = []
    for h in range(MEM_HEADS):
        sl = slice(h * MEM_HEAD_DIM, (h + 1) * MEM_HEAD_DIM)
        logits = _dot_t(qb[:, sl], km[:, sl]) * (MEM_HEAD_DIM ** -0.5)
        m = jnp.max(logits, axis=1, keepdims=True)
        pexp = jnp.exp(logits - m)
        denom = jnp.sum(pexp, axis=1, keepdims=True)
        outs.append((_dot(pexp.astype(BF16), vm[:, sl]) / denom).astype(BF16))
    o = jnp.concatenate(outs, axis=1)
    h_ref[...] = h1 + _dot(o, wco_ref[...])


def _mix_call(x2, osb, odsa, g1, wg, bg, wbs, wbd, wo, g2, wcq, km, vm, wco, seq):
    n = x2.shape[0]
    tm = ROW_TILE
    tiles_per_seq = seq // tm
    half = MEM_HEADS * MEM_HEAD_DIM
    row = lambda i: (i, 0)
    const = lambda i: (0, 0)
    bat = lambda i: (i // tiles_per_seq, 0)

    def full(a):
        return pl.BlockSpec(a.shape, const)

    return pl.pallas_call(
        _mix_kernel,
        grid=(n // tm,),
        in_specs=[
            pl.BlockSpec((tm, D_MODEL), row),
            pl.BlockSpec((tm, 512), row),
            pl.BlockSpec((tm, 512), row),
            full(g1), full(wg), full(bg), full(wbs), full(wbd), full(wo),
            full(g2), full(wcq),
            pl.BlockSpec((N_MEM, half), bat),
            pl.BlockSpec((N_MEM, half), bat),
            full(wco),
        ],
        out_specs=pl.BlockSpec((tm, D_MODEL), row),
        out_shape=jax.ShapeDtypeStruct((n, D_MODEL), F32),
        compiler_params=pltpu.CompilerParams(
            dimension_semantics=("arbitrary",), vmem_limit_bytes=VMEM_LIMIT),
        name="mix",
    )(x2, osb, odsa, g1, wg, bg, wbs, wbd, wo, g2, wcq, km, vm, wco)


def _mlp_kernel(h_ref, g_ref, wu_ref, wd_ref, gf_ref, o_ref, *, final_norm):
    h = h_ref[...]
    ub = _rms(h, g_ref[...]).astype(BF16)
    acc = h
    for c in range(D_FF // D_MODEL):
        sl = slice(c * D_MODEL, (c + 1) * D_MODEL)
        hid = jnp.maximum(_dot(ub, wu_ref[:, sl]), 0.0)
        acc = acc + _dot((hid * hid).astype(BF16), wd_ref[sl, :])
    if final_norm:
        acc = _rms(acc, gf_ref[...])
    o_ref[...] = acc


def _mlp_call(h2, g, wu, wd, gf, final_norm):
    n = h2.shape[0]
    tm = ROW_TILE
    row = lambda i: (i, 0)
    const = lambda i: (0, 0)
    return pl.pallas_call(
        functools.partial(_mlp_kernel, final_norm=final_norm),
        grid=(n // tm,),
        in_specs=[
            pl.BlockSpec((tm, D_MODEL), row),
            pl.BlockSpec((1, D_MODEL), const),
            pl.BlockSpec((D_MODEL, D_FF), const),
            pl.BlockSpec((D_FF, D_MODEL), const),
            pl.BlockSpec((1, D_MODEL), const),
        ],
        out_specs=pl.BlockSpec((tm, D_MODEL), row),
        out_shape=jax.ShapeDtypeStruct((n, D_MODEL), F32),
        compiler_params=pltpu.CompilerParams(
            dimension_semantics=("arbitrary",), vmem_limit_bytes=VMEM_LIMIT),
        name="mlp",
    )(h2, g, wu, wd, gf)


def _pack_w_in(w_in):
    offs = np.cumsum([0, 512, 512, 512, 512, 128, 128, 512, 64, 8])
    sbq, sbk, sbv, dq, dk, dv, iq, ik, iw = [w_in[:, offs[j]:offs[j + 1]] for j in range(9)]
    dq = dq.reshape(D_MODEL, 2, 4, HEAD_DIM).transpose(0, 2, 1, 3).reshape(D_MODEL, 512)
    iw = jnp.pad(iw, ((0, 0), (0, LANES - IDX_HEADS)))
    return jnp.concatenate([sbq, sbk, sbv, dq, dk, dv, iq, ik, ik, iw], axis=1).astype(BF16)


def _rope_tables(seq):
    inv = ROPE_THETA ** (-jnp.arange(0, HEAD_DIM, 2, dtype=F32) / HEAD_DIM)
    ang = jnp.arange(seq).astype(F32)[:, None] * inv[None, :]
    cos = jnp.cos(ang)
    sin = jnp.sin(ang)
    cos_t = jnp.tile(cos, (1, LANES // HALF_ROT))
    sin_t = jnp.tile(jnp.concatenate([-sin, sin], axis=1), (1, LANES // HEAD_DIM))
    return cos_t, sin_t


def kernel(x, mem, norm_mix, w_in, w_branch_sb, w_branch_dsa, w_gate, b_gate, w_out, norm_cross,
           norm_mem, w_cq, w_ckv, w_co, norm_mlp, w_up, w_down, norm_final):
    batch, seq, d = x.shape
    depth = w_in.shape[0]
    cos_t, sin_t = _rope_tables(seq)
    h = x.reshape(batch * seq, d)
    mem2 = mem.reshape(batch * mem.shape[1], d)
    gf = norm_final.reshape(1, d)
    for l in range(depth):
        w_packed = _pack_w_in(w_in[l])
        wbd = (w_branch_dsa[l].reshape(2, 4, HEAD_DIM, d).transpose(1, 0, 2, 3)
               .reshape(DSA_HEADS * HEAD_DIM, d).astype(BF16))
        sbq, sbk, sbv, dq, dk, dv, iq, ik, iw = _proj_call(
            h, norm_mix[l].reshape(1, d), w_packed, cos_t, sin_t, seq)
        o_sb = _sb_call(sbq, sbk, sbv, batch, seq)
        o_dsa = _dsa_call(dq, iq, iw, dk, dv, ik, batch, seq)
        km, vm = _memkv_call(mem2, norm_mem[l].reshape(1, d), w_ckv[l].astype(BF16), batch)
        h2 = _mix_call(h, o_sb, o_dsa, norm_mix[l].reshape(1, d), w_gate[l].astype(BF16),
                       b_gate[l].reshape(1, -1), w_branch_sb[l].astype(BF16), wbd,
                       w_out[l].astype(BF16), norm_cross[l].reshape(1, d), w_cq[l].astype(BF16),
                       km, vm, w_co[l].astype(BF16), seq)
        h = _mlp_call(h2, norm_mlp[l].reshape(1, d), w_up[l].astype(BF16), w_down[l].astype(BF16),
                      gf, final_norm=(l == depth - 1))
    return h.reshape(batch, seq, d)
```

```python
import functools

import jax
import jax.numpy as jnp
import numpy as np
from jax import lax
from jax.experimental import pallas as pl
from jax.experimental.pallas import tpu as pltpu

F32 = jnp.float32
BF16 = jnp.bfloat16

D_MODEL = 1024
N_MEM = 256
SB_HEADS = 8
DSA_HEADS = 8
DSA_KV_HEADS = 2
IDX_HEADS = 8
HEAD_DIM = 64
TOPK_MAX = 256
MEM_HEADS = 4
MEM_HEAD_DIM = 128
D_FF = 4 * D_MODEL
ROPE_THETA = 10000.0
EPS = 1e-6

LANES = 128
HALF_ROT = HEAD_DIM // 2

W_SB = SB_HEADS * HEAD_DIM
OFF_SBQ = 0
OFF_SBK = OFF_SBQ + W_SB
OFF_SBV = OFF_SBK + W_SB
OFF_DQ = OFF_SBV + W_SB
OFF_DK = OFF_DQ + DSA_HEADS * HEAD_DIM
OFF_DV = OFF_DK + DSA_KV_HEADS * HEAD_DIM
OFF_IQ = OFF_DV + DSA_KV_HEADS * HEAD_DIM
OFF_IK = OFF_IQ + IDX_HEADS * HEAD_DIM
OFF_IW = OFF_IK + LANES
D_IN_PACKED = OFF_IW + LANES

ROW_TILE = 512
SB_TILE = 256
DSA_TQ = 256
DSA_KCHUNK = 512
VMEM_LIMIT = 56 * 1024 * 1024
NEG_BIG = -1e30
DSA_BISECT_STEPS = 36
LOG2_E = 1.4426950408889634


def _rms(x, g):
    return x * lax.rsqrt(jnp.mean(x * x, axis=-1, keepdims=True) + EPS) * g


def _dot(a, b):
    return jnp.dot(a, b, preferred_element_type=F32)


def _dot_t(a, b):
    return lax.dot_general(a, b, (((1,), (1,)), ((), ())), preferred_element_type=F32)


def _proj_kernel(x_ref, g_ref, w_ref, cos_ref, sin_ref,
                 sbq_ref, sbk_ref, sbv_ref, dq_ref, dk_ref, dv_ref, iq_ref, ik_ref, iw_ref):
    ub = _rms(x_ref[...], g_ref[...]).astype(BF16)
    cos = cos_ref[...]
    sin = sin_ref[...]
    lane = lax.broadcasted_iota(jnp.int32, (1, LANES), 1)
    first_half = (lane % HEAD_DIM) < HALF_ROT

    def mm(off, width=LANES):
        return _dot(ub, w_ref[:, off:off + width])

    def rope(s):
        rot = jnp.where(first_half, pltpu.roll(s, LANES - HALF_ROT, 1), pltpu.roll(s, HALF_ROT, 1))
        return s * cos + rot * sin

    q_scale = HEAD_DIM ** -0.5
    sbq_ref[...] = (mm(OFF_SBQ, W_SB) * (q_scale * LOG2_E)).astype(BF16)
    sbk_ref[...] = mm(OFF_SBK, W_SB).astype(BF16)
    sbv_ref[...] = mm(OFF_SBV, W_SB).astype(BF16)
    for p in range(4):
        sl = slice(p * LANES, (p + 1) * LANES)
        dq_ref[:, sl] = (rope(mm(OFF_DQ + p * LANES)) * q_scale).astype(BF16)
        iq_ref[:, sl] = (rope(mm(OFF_IQ + p * LANES)) * q_scale).astype(BF16)
    dk_ref[...] = rope(mm(OFF_DK)).astype(BF16)
    dv_ref[...] = mm(OFF_DV).astype(BF16)
    ik_ref[...] = rope(mm(OFF_IK)).astype(BF16)
    iw_ref[...] = mm(OFF_IW) * (IDX_HEADS ** -0.5)


def _proj_call(x2, g, w_packed, cos_t, sin_t, seq):
    n = x2.shape[0]
    tm = ROW_TILE
    tiles_per_seq = seq // tm
    row = lambda i: (i, 0)
    const = lambda i: (0, 0)
    pos = lambda i: (i % tiles_per_seq, 0)
    out_w = [W_SB, W_SB, W_SB, 512, LANES, LANES, 512, LANES, LANES]
    out_dt = [BF16] * 8 + [F32]
    return pl.pallas_call(
        _proj_kernel,
        grid=(n // tm,),
        in_specs=[
            pl.BlockSpec((tm, D_MODEL), row),
            pl.BlockSpec((1, D_MODEL), const),
            pl.BlockSpec((D_MODEL, D_IN_PACKED), const),
            pl.BlockSpec((tm, LANES), pos),
            pl.BlockSpec((tm, LANES), pos),
        ],
        out_specs=[pl.BlockSpec((tm, w), row) for w in out_w],
        out_shape=[jax.ShapeDtypeStruct((n, w), dt) for w, dt in zip(out_w, out_dt)],
        compiler_params=pltpu.CompilerParams(
            dimension_semantics=("arbitrary",), vmem_limit_bytes=VMEM_LIMIT),
        name="proj",
    )(x2, g, w_packed, cos_t, sin_t)


def _sb_kernel(q_ref, k_ref, v_ref, o_ref, acc_sc):
    t = SB_TILE
    i = pl.program_id(1)
    lane = lax.broadcasted_iota(jnp.int32, (1, LANES), 1)
    first_head = lane < HEAD_DIM
    row = lax.broadcasted_iota(jnp.int32, (t, t), 0)
    col = lax.broadcasted_iota(jnp.int32, (t, t), 1)
    strict = col < row
    later = jnp.where(row > col, 1.0, 0.0).astype(BF16)
    zero_q = jnp.zeros((t, LANES), BF16)
    qh = []
    for p in range(SB_HEADS // 2):
        slab = q_ref[:, p * LANES:(p + 1) * LANES]
        qh.append(jnp.where(first_head, slab, zero_q))
        qh.append(jnp.where(first_head, zero_q, slab))

    def tile(start, diag, carries):
        new = []
        for p in range(SB_HEADS // 2):
            sl = slice(p * LANES, (p + 1) * LANES)
            kb = k_ref[pl.ds(start, t), sl]
            vb = v_ref[pl.ds(start, t), sl]
            av = []
            for h in range(2):
                z = _dot_t(qh[2 * p + h], kb)
                sp = jnp.log2(1.0 + jnp.exp2(-jnp.abs(z)))
                log_beta = jnp.minimum(z, 0.0) - sp
                neg_log_1mb = jnp.maximum(z, 0.0) + sp
                if diag:
                    neg_log_1mb = jnp.where(strict, neg_log_1mb, 0.0)
                suffix = _dot(neg_log_1mb.astype(BF16), later) + carries[2 * p + h]
                a = jnp.exp2(log_beta - suffix)
                if diag:
                    a = jnp.where(strict, a, 0.0)
                av.append(_dot(a.astype(BF16), vb))
                new.append(carries[2 * p + h] + jnp.sum(neg_log_1mb, axis=1, keepdims=True))
            contrib = jnp.where(first_head, av[0], av[1])
            if diag:
                acc_sc[:, sl] = contrib
            else:
                acc_sc[:, sl] += contrib
        return tuple(new)

    zc = jnp.zeros((t, 1), F32)
    carries = tile(pl.multiple_of(i * t, t), True, (zc,) * SB_HEADS)

    def body(s, carries):
        return tile(pl.multiple_of((i - 1 - s) * t, t), False, carries)

    lax.fori_loop(0, i, body, carries)
    o_ref[...] = acc_sc[...].astype(BF16)


def _sb_call(q, k, v, batch, seq):
    t = SB_TILE
    nq = seq // t
    qmap = lambda b, i: (b * nq + i, 0)
    kvmap = lambda b, i: (b, 0)
    return pl.pallas_call(
        _sb_kernel,
        grid=(batch, nq),
        in_specs=[
            pl.BlockSpec((t, W_SB), qmap),
            pl.BlockSpec((seq, W_SB), kvmap),
            pl.BlockSpec((seq, W_SB), kvmap),
        ],
        out_specs=pl.BlockSpec((t, W_SB), qmap),
        out_shape=jax.ShapeDtypeStruct((batch * seq, W_SB), BF16),
        scratch_shapes=[pltpu.VMEM((t, W_SB), F32)],
        compiler_params=pltpu.CompilerParams(
            dimension_semantics=("arbitrary", "arbitrary"), vmem_limit_bytes=VMEM_LIMIT),
        name="stickbreak",
    )(q, k, v)


def _dsa_tile(width, i, dq_ref, iq_ref, iw_ref, dk_ref, dv_ref, ik_ref, o_ref, score_sc, bias_sc):
    tq = DSA_TQ
    lane = lax.broadcasted_iota(jnp.int32, (1, LANES), 1)
    half = lane // HEAD_DIM
    t_idx = i * tq + lax.broadcasted_iota(jnp.int32, (tq, 1), 0)

    iq = iq_ref[...]
    zero_q = jnp.zeros((tq, LANES), BF16)
    qstack = jnp.concatenate(
        [jnp.where(half == (h % 2), iq[:, (h // 2) * LANES:(h // 2 + 1) * LANES], zero_q)
         for h in range(IDX_HEADS)], axis=0)
    iw = iw_ref[...]
    wcol = [iw[:, h:h + 1] for h in range(IDX_HEADS)]
    lo = hi = None
    for c in range(width // DSA_KCHUNK):
        ksl = slice(c * DSA_KCHUNK, (c + 1) * DSA_KCHUNK)
        dots = _dot_t(qstack, ik_ref[ksl, :])
        score = jnp.zeros((tq, DSA_KCHUNK), F32)
        for h in range(IDX_HEADS):
            score = score + jnp.maximum(dots[h * tq:(h + 1) * tq], 0.0) * wcol[h]
        kpos = c * DSA_KCHUNK + lax.broadcasted_iota(jnp.int32, (1, DSA_KCHUNK), 1)
        causal = kpos <= t_idx
        score_sc[:, ksl] = jnp.where(causal, score, -jnp.inf)
        cmin = jnp.min(jnp.where(causal, score, jnp.inf), axis=1, keepdims=True)
        cmax = jnp.max(jnp.where(causal, score, -jnp.inf), axis=1, keepdims=True)
        lo = cmin if c == 0 else jnp.minimum(lo, cmin)
        hi = cmax if c == 0 else jnp.maximum(hi, cmax)

    k_row = jnp.minimum(t_idx + 1, TOPK_MAX).astype(F32)
    hi = hi + (jnp.abs(hi) + 1.0)

    def count(pred):
        return jnp.sum(jnp.where(pred, 1.0, 0.0), axis=1, keepdims=True)

    def bisect(it, bracket):
        lo, hi = bracket
        mid = lo + 0.5 * (hi - lo)
        ge = count(score_sc[:, :width] >= mid) >= k_row
        return jnp.where(ge, mid, lo), jnp.where(ge, hi, mid)

    lo, hi = lax.fori_loop(0, DSA_BISECT_STEPS, bisect, (lo, hi), unroll=4)

    need = k_row - count(score_sc[:, :width] >= hi)
    surplus = jnp.max(count(score_sc[:, :width] >= lo) - k_row)

    @pl.when(surplus <= 0.0)
    def _():
        bias_sc[:, :width] = jnp.where(score_sc[:, :width] >= lo, 0.0, NEG_BIG)

    @pl.when(surplus > 0.0)
    def _():
        r128 = lax.broadcasted_iota(jnp.int32, (LANES, LANES), 0)
        c128 = lax.broadcasted_iota(jnp.int32, (LANES, LANES), 1)
        earlier = jnp.where(r128 < c128, 1.0, 0.0).astype(BF16)
        ties_before = jnp.zeros((tq, 1), F32)
        for c in range(width // LANES):
            ksl = slice(c * LANES, (c + 1) * LANES)
            sc = score_sc[:, ksl]
            tied = jnp.where(sc >= lo, jnp.where(sc < hi, 1.0, 0.0), 0.0)
            rank = _dot(tied.astype(BF16), earlier) + ties_before
            tie_bias = jnp.where(rank < need, 0.0, NEG_BIG)
            bias_sc[:, ksl] = jnp.where(sc >= hi, 0.0, jnp.where(sc >= lo, tie_bias, NEG_BIG))
            ties_before = ties_before + jnp.sum(tied, axis=1, keepdims=True)

    dq = dq_ref[...]
    heads_per_group = DSA_HEADS // DSA_KV_HEADS
    outs = []
    for g in range(DSA_KV_HEADS):
        qg = jnp.concatenate(
            [jnp.where(half == g, dq[:, p * LANES:(p + 1) * LANES], zero_q)
             for p in range(heads_per_group)], axis=0)
        m = denom = acc = None
        for c in range(width // DSA_KCHUNK):
            ksl = slice(c * DSA_KCHUNK, (c + 1) * DSA_KCHUNK)
            logits = _dot_t(qg, dk_ref[ksl, :])
            logits = jnp.concatenate(
                [logits[p * tq:(p + 1) * tq] + bias_sc[:, ksl] for p in range(heads_per_group)], axis=0)
            cmax = jnp.max(logits, axis=1, keepdims=True)
            if c == 0:
                m = cmax
                pexp = jnp.exp(logits - m)
                denom = jnp.sum(pexp, axis=1, keepdims=True)
                acc = _dot(pexp.astype(BF16), dv_ref[ksl, :])
            else:
                m_new = jnp.maximum(m, cmax)
                alpha = jnp.exp(m - m_new)
                pexp = jnp.exp(logits - m_new)
                denom = alpha * denom + jnp.sum(pexp, axis=1, keepdims=True)
                acc = alpha * acc + _dot(pexp.astype(BF16), dv_ref[ksl, :])
                m = m_new
        outs.append(acc / denom)
    for p in range(heads_per_group):
        rows = slice(p * tq, (p + 1) * tq)
        o_ref[:, p * LANES:(p + 1) * LANES] = jnp.where(half == 0, outs[0][rows], outs[1][rows]).astype(BF16)


def _dsa_kernel(dq_ref, iq_ref, iw_ref, dk_ref, dv_ref, ik_ref, o_ref, score_sc, bias_sc, *, seq):
    i = pl.program_id(1)
    tiles_per_width = DSA_KCHUNK // DSA_TQ
    for v in range(seq // DSA_KCHUNK):
        @pl.when(i // tiles_per_width == v)
        def _(v=v):
            _dsa_tile((v + 1) * DSA_KCHUNK, i, dq_ref, iq_ref, iw_ref, dk_ref, dv_ref, ik_ref,
                      o_ref, score_sc, bias_sc)


def _dsa_call(dq, iq, iw, dk, dv, ik, batch, seq):
    tq = DSA_TQ
    nq = seq // tq
    qmap = lambda b, i: (b * nq + i, 0)
    kvmap = lambda b, i: (b, 0)
    return pl.pallas_call(
        functools.partial(_dsa_kernel, seq=seq),
        grid=(batch, nq),
        in_specs=[
            pl.BlockSpec((tq, 512), qmap),
            pl.BlockSpec((tq, 512), qmap),
            pl.BlockSpec((tq, LANES), qmap),
            pl.BlockSpec((seq, LANES), kvmap),
            pl.BlockSpec((seq, LANES), kvmap),
            pl.BlockSpec((seq, LANES), kvmap),
        ],
        out_specs=pl.BlockSpec((tq, 512), qmap),
        out_shape=jax.ShapeDtypeStruct((batch * seq, 512), BF16),
        scratch_shapes=[pltpu.VMEM((tq, seq), F32), pltpu.VMEM((tq, seq), F32)],
        compiler_params=pltpu.CompilerParams(
            dimension_semantics=("arbitrary", "arbitrary"), vmem_limit_bytes=VMEM_LIMIT),
        name="dsa",
    )(dq, iq, iw, dk, dv, ik)


def _memkv_kernel(m_ref, g_ref, w_ref, k_ref, v_ref):
    mb = _rms(m_ref[...], g_ref[...]).astype(BF16)
    kv = _dot(mb, w_ref[...])
    half = MEM_HEADS * MEM_HEAD_DIM
    k_ref[...] = kv[:, :half].astype(BF16)
    v_ref[...] = kv[:, half:].astype(BF16)


def _memkv_call(mem2, g, w_ckv, batch):
    half = MEM_HEADS * MEM_HEAD_DIM
    row = lambda b: (b, 0)
    const = lambda b: (0, 0)
    return pl.pallas_call(
        _memkv_kernel,
        grid=(batch,),
        in_specs=[
            pl.BlockSpec((N_MEM, D_MODEL), row),
            pl.BlockSpec((1, D_MODEL), const),
            pl.BlockSpec((D_MODEL, 2 * half), const),
        ],
        out_specs=[pl.BlockSpec((N_MEM, half), row)] * 2,
        out_shape=[jax.ShapeDtypeStruct((batch * N_MEM, half), BF16)] * 2,
        compiler_params=pltpu.CompilerParams(
            dimension_semantics=("arbitrary",), vmem_limit_bytes=VMEM_LIMIT),
        name="memkv",
    )(mem2, g, w_ckv)


def _mix_kernel(x_ref, osb_ref, odsa_ref, g1_ref, wg_ref, bg_ref, wbs_ref, wbd_ref, wo_ref,
                g2_ref, wcq_ref, km_ref, vm_ref, wco_ref, h_ref):
    x = x_ref[...]
    ub = _rms(x, g1_ref[...]).astype(BF16)
    gates = 1.0 / (1.0 + jnp.exp(-(_dot(ub, wg_ref[...]) + bg_ref[...])))
    merged = (gates[:, :D_MODEL] * _dot(osb_ref[...], wbs_ref[...])
              + gates[:, D_MODEL:] * _dot(odsa_ref[...], wbd_ref[...]))
    h1 = x + _dot(merged.astype(BF16), wo_ref[...])

    u2 = _rms(h1, g2_ref[...]).astype(BF16)
    qb = _dot(u2, wcq_ref[...]).astype(BF16)
    km = km_ref[...]
    vm = vm_ref[...]
    outs = []
    for h in range(MEM_HEADS):
        sl = slice(h * MEM_HEAD_DIM, (h + 1) * MEM_HEAD_DIM)
        logits = _dot_t(qb[:, sl], km[:, sl]) * (MEM_HEAD_DIM ** -0.5)
        m = jnp.max(logits, axis=1, keepdims=True)
        pexp = jnp.exp(logits - m)
        denom = jnp.sum(pexp, axis=1, keepdims=True)
        outs.append((_dot(pexp.astype(BF16), vm[:, sl]) / denom).astype(BF16))
    o = jnp.concatenate(outs, axis=1)
    h_ref[...] = h1 + _dot(o, wco_ref[...])


def _mix_call(x2, osb, odsa, g1, wg, bg, wbs, wbd, wo, g2, wcq, km, vm, wco, seq):
    n = x2.shape[0]
    tm = ROW_TILE
    tiles_per_seq = seq // tm
    half = MEM_HEADS * MEM_HEAD_DIM
    row = lambda i: (i, 0)
    const = lambda i: (0, 0)
    bat = lambda i: (i // tiles_per_seq, 0)

    def full(a):
        return pl.BlockSpec(a.shape, const)

    return pl.pallas_call(
        _mix_kernel,
        grid=(n // tm,),
        in_specs=[
            pl.BlockSpec((tm, D_MODEL), row),
            pl.BlockSpec((tm, 512), row),
            pl.BlockSpec((tm, 512), row),
            full(g1), full(wg), full(bg), full(wbs), full(wbd), full(wo),
            full(g2), full(wcq),
            pl.BlockSpec((N_MEM, half), bat),
            pl.BlockSpec((N_MEM, half), bat),
            full(wco),
        ],
        out_specs=pl.BlockSpec((tm, D_MODEL), row),
        out_shape=jax.ShapeDtypeStruct((n, D_MODEL), F32),
        compiler_params=pltpu.CompilerParams(
            dimension_semantics=("arbitrary",), vmem_limit_bytes=VMEM_LIMIT),
        name="mix",
    )(x2, osb, odsa, g1, wg, bg, wbs, wbd, wo, g2, wcq, km, vm, wco)


def _mlp_kernel(h_ref, g_ref, wu_ref, wd_ref, gf_ref, o_ref, *, final_norm):
    h = h_ref[...]
    ub = _rms(h, g_ref[...]).astype(BF16)
    acc = h
    for c in range(D_FF // D_MODEL):
        sl = slice(c * D_MODEL, (c + 1) * D_MODEL)
        hid = jnp.maximum(_dot(ub, wu_ref[:, sl]), 0.0)
        acc = acc + _dot((hid * hid).astype(BF16), wd_ref[sl, :])
    if final_norm:
        acc = _rms(acc, gf_ref[...])
    o_ref[...] = acc


def _mlp_call(h2, g, wu, wd, gf, final_norm):
    n = h2.shape[0]
    tm = ROW_TILE
    row = lambda i: (i, 0)
    const = lambda i: (0, 0)
    return pl.pallas_call(
        functools.partial(_mlp_kernel, final_norm=final_norm),
        grid=(n // tm,),
        in_specs=[
            pl.BlockSpec((tm, D_MODEL), row),
            pl.BlockSpec((1, D_MODEL), const),
            pl.BlockSpec((D_MODEL, D_FF), const),
            pl.BlockSpec((D_FF, D_MODEL), const),
            pl.BlockSpec((1, D_MODEL), const),
        ],
        out_specs=pl.BlockSpec((tm, D_MODEL), row),
        out_shape=jax.ShapeDtypeStruct((n, D_MODEL), F32),
        compiler_params=pltpu.CompilerParams(
            dimension_semantics=("arbitrary",), vmem_limit_bytes=VMEM_LIMIT),
        name="mlp",
    )(h2, g, wu, wd, gf)


def _pack_w_in(w_in):
    offs = np.cumsum([0, 512, 512, 512, 512, 128, 128, 512, 64, 8])
    sbq, sbk, sbv, dq, dk, dv, iq, ik, iw = [w_in[:, offs[j]:offs[j + 1]] for j in range(9)]
    dq = dq.reshape(D_MODEL, 2, 4, HEAD_DIM).transpose(0, 2, 1, 3).reshape(D_MODEL, 512)
    iw = jnp.pad(iw, ((0, 0), (0, LANES - IDX_HEADS)))
    return jnp.concatenate([sbq, sbk, sbv, dq, dk, dv, iq, ik, ik, iw], axis=1).astype(BF16)


def _rope_tables(seq):
    inv = ROPE_THETA ** (-jnp.arange(0, HEAD_DIM, 2, dtype=F32) / HEAD_DIM)
    ang = jnp.arange(seq).astype(F32)[:, None] * inv[None, :]
    cos = jnp.cos(ang)
    sin = jnp.sin(ang)
    cos_t = jnp.tile(cos, (1, LANES // HALF_ROT))
    sin_t = jnp.tile(jnp.concatenate([-sin, sin], axis=1), (1, LANES // HEAD_DIM))
    return cos_t, sin_t


def kernel(x, mem, norm_mix, w_in, w_branch_sb, w_branch_dsa, w_gate, b_gate, w_out, norm_cross,
           norm_mem, w_cq, w_ckv, w_co, norm_mlp, w_up, w_down, norm_final):
    batch, seq, d = x.shape
    depth = w_in.shape[0]
    cos_t, sin_t = _rope_tables(seq)
    h = x.reshape(batch * seq, d)
    mem2 = mem.reshape(batch * mem.shape[1], d)
    gf = norm_final.reshape(1, d)
    for l in range(depth):
        w_packed = _pack_w_in(w_in[l])
        wbd = (w_branch_dsa[l].reshape(2, 4, HEAD_DIM, d).transpose(1, 0, 2, 3)
               .reshape(DSA_HEADS * HEAD_DIM, d).astype(BF16))
        sbq, sbk, sbv, dq, dk, dv, iq, ik, iw = _proj_call(
            h, norm_mix[l].reshape(1, d), w_packed, cos_t, sin_t, seq)
        o_sb = _sb_call(sbq, sbk, sbv, batch, seq)
        o_dsa = _dsa_call(dq, iq, iw, dk, dv, ik, batch, seq)
        km, vm = _memkv_call(mem2, norm_mem[l].reshape(1, d), w_ckv[l].astype(BF16), batch)
        h2 = _mix_call(h, o_sb, o_dsa, norm_mix[l].reshape(1, d), w_gate[l].astype(BF16),
                       b_gate[l].reshape(1, -1), w_branch_sb[l].astype(BF16), wbd,
                       w_out[l].astype(BF16), norm_cross[l].reshape(1, d), w_cq[l].astype(BF16),
                       km, vm, w_co[l].astype(BF16), seq)
        h = _mlp_call(h2, norm_mlp[l].reshape(1, d), w_up[l].astype(BF16), w_down[l].astype(BF16),
                      gf, final_norm=(l == depth - 1))
    return h.reshape(batch, seq, d)
```

```python
import functools

import jax
import jax.numpy as jnp
import numpy as np
from jax import lax
from jax.experimental import pallas as pl
from jax.experimental.pallas import tpu as pltpu

F32 = jnp.float32
BF16 = jnp.bfloat16

D_MODEL = 1024
N_MEM = 256
SB_HEADS = 8
DSA_HEADS = 8
DSA_KV_HEADS = 2
IDX_HEADS = 8
HEAD_DIM = 64
TOPK_MAX = 256
MEM_HEADS = 4
MEM_HEAD_DIM = 128
D_FF = 4 * D_MODEL
ROPE_THETA = 10000.0
EPS = 1e-6

LANES = 128
HALF_ROT = HEAD_DIM // 2

W_SB = SB_HEADS * HEAD_DIM
OFF_SBQ = 0
OFF_SBK = OFF_SBQ + W_SB
OFF_SBV = OFF_SBK + W_SB
OFF_DQ = OFF_SBV + W_SB
OFF_DK = OFF_DQ + DSA_HEADS * HEAD_DIM
OFF_DV = OFF_DK + DSA_KV_HEADS * HEAD_DIM
OFF_IQ = OFF_DV + DSA_KV_HEADS * HEAD_DIM
OFF_IK = OFF_IQ + IDX_HEADS * HEAD_DIM
OFF_IW = OFF_IK + LANES
D_IN_PACKED = OFF_IW + LANES

ROW_TILE = 512
SB_TILE = 256
DSA_TQ = 256
DSA_KCHUNK = 512
VMEM_LIMIT = 56 * 1024 * 1024
NEG_BIG = -1e30
DSA_BISECT_STEPS = 24
DSA_BISECT_UNROLL = 4
DSA_BISECT_MAX_TRIPS = 96
LOG2_E = 1.4426950408889634


def _rms(x, g):
    return x * lax.rsqrt(jnp.mean(x * x, axis=-1, keepdims=True) + EPS) * g


def _dot(a, b):
    return jnp.dot(a, b, preferred_element_type=F32)


def _dot_t(a, b):
    return lax.dot_general(a, b, (((1,), (1,)), ((), ())), preferred_element_type=F32)


def _proj_kernel(x_ref, g_ref, w_ref, cos_ref, sin_ref,
                 sbq_ref, sbk_ref, sbv_ref, dq_ref, dk_ref, dv_ref, iq_ref, ik_ref, iw_ref):
    ub = _rms(x_ref[...], g_ref[...]).astype(BF16)
    cos = cos_ref[...]
    sin = sin_ref[...]
    lane = lax.broadcasted_iota(jnp.int32, (1, LANES), 1)
    first_half = (lane % HEAD_DIM) < HALF_ROT

    def mm(off, width=LANES):
        return _dot(ub, w_ref[:, off:off + width])

    def rope(s):
        rot = jnp.where(first_half, pltpu.roll(s, LANES - HALF_ROT, 1), pltpu.roll(s, HALF_ROT, 1))
        return s * cos + rot * sin

    q_scale = HEAD_DIM ** -0.5
    sbq_ref[...] = (mm(OFF_SBQ, W_SB) * (q_scale * LOG2_E)).astype(BF16)
    sbk_ref[...] = mm(OFF_SBK, W_SB).astype(BF16)
    sbv_ref[...] = mm(OFF_SBV, W_SB).astype(BF16)
    for p in range(4):
        sl = slice(p * LANES, (p + 1) * LANES)
        dq_ref[:, sl] = (rope(mm(OFF_DQ + p * LANES)) * q_scale).astype(BF16)
        iq_ref[:, sl] = (rope(mm(OFF_IQ + p * LANES)) * q_scale).astype(BF16)
    dk_ref[...] = rope(mm(OFF_DK)).astype(BF16)
    dv_ref[...] = mm(OFF_DV).astype(BF16)
    ik_ref[...] = rope(mm(OFF_IK)).astype(BF16)
    iw_ref[...] = mm(OFF_IW) * (IDX_HEADS ** -0.5)


def _proj_call(x2, g, w_packed, cos_t, sin_t, seq):
    n = x2.shape[0]
    tm = ROW_TILE
    tiles_per_seq = seq // tm
    row = lambda i: (i, 0)
    const = lambda i: (0, 0)
    pos = lambda i: (i % tiles_per_seq, 0)
    out_w = [W_SB, W_SB, W_SB, 512, LANES, LANES, 512, LANES, LANES]
    out_dt = [BF16] * 8 + [F32]
    return pl.pallas_call(
        _proj_kernel,
        grid=(n // tm,),
        in_specs=[
            pl.BlockSpec((tm, D_MODEL), row),
            pl.BlockSpec((1, D_MODEL), const),
            pl.BlockSpec((D_MODEL, D_IN_PACKED), const),
            pl.BlockSpec((tm, LANES), pos),
            pl.BlockSpec((tm, LANES), pos),
        ],
        out_specs=[pl.BlockSpec((tm, w), row) for w in out_w],
        out_shape=[jax.ShapeDtypeStruct((n, w), dt) for w, dt in zip(out_w, out_dt)],
        compiler_params=pltpu.CompilerParams(
            dimension_semantics=("arbitrary",), vmem_limit_bytes=VMEM_LIMIT),
        name="proj",
    )(x2, g, w_packed, cos_t, sin_t)


def _sb_kernel(q_ref, k_ref, v_ref, o_ref, acc_sc):
    t = SB_TILE
    i = pl.program_id(1)
    lane = lax.broadcasted_iota(jnp.int32, (1, LANES), 1)
    first_head = lane < HEAD_DIM
    row = lax.broadcasted_iota(jnp.int32, (t, t), 0)
    col = lax.broadcasted_iota(jnp.int32, (t, t), 1)
    strict = col < row
    later = jnp.where(row > col, 1.0, 0.0).astype(BF16)
    zero_q = jnp.zeros((t, LANES), BF16)
    qh = []
    for p in range(SB_HEADS // 2):
        slab = q_ref[:, p * LANES:(p + 1) * LANES]
        qh.append(jnp.where(first_head, slab, zero_q))
        qh.append(jnp.where(first_head, zero_q, slab))

    def tile(start, diag, carries):
        new = []
        for p in range(SB_HEADS // 2):
            sl = slice(p * LANES, (p + 1) * LANES)
            kb = k_ref[pl.ds(start, t), sl]
            vb = v_ref[pl.ds(start, t), sl]
            av = []
            for h in range(2):
                z = _dot_t(qh[2 * p + h], kb)
                sp = jnp.log2(1.0 + jnp.exp2(-jnp.abs(z)))
                log_beta = jnp.minimum(z, 0.0) - sp
                neg_log_1mb = jnp.maximum(z, 0.0) + sp
                if diag:
                    neg_log_1mb = jnp.where(strict, neg_log_1mb, 0.0)
                suffix = _dot(neg_log_1mb.astype(BF16), later) + carries[2 * p + h]
                a = jnp.exp2(log_beta - suffix)
                if diag:
                    a = jnp.where(strict, a, 0.0)
                av.append(_dot(a.astype(BF16), vb))
                new.append(carries[2 * p + h] + jnp.sum(neg_log_1mb, axis=1, keepdims=True))
            contrib = jnp.where(first_head, av[0], av[1])
            if diag:
                acc_sc[:, sl] = contrib
            else:
                acc_sc[:, sl] += contrib
        return tuple(new)

    zc = jnp.zeros((t, 1), F32)
    carries = tile(pl.multiple_of(i * t, t), True, (zc,) * SB_HEADS)

    def body(s, carries):
        return tile(pl.multiple_of((i - 1 - s) * t, t), False, carries)

    lax.fori_loop(0, i, body, carries)
    o_ref[...] = acc_sc[...].astype(BF16)


def _sb_call(q, k, v, batch, seq):
    t = SB_TILE
    nq = seq // t
    qmap = lambda b, i: (b * nq + i, 0)
    kvmap = lambda b, i: (b, 0)
    return pl.pallas_call(
        _sb_kernel,
        grid=(batch, nq),
        in_specs=[
            pl.BlockSpec((t, W_SB), qmap),
            pl.BlockSpec((seq, W_SB), kvmap),
            pl.BlockSpec((seq, W_SB), kvmap),
        ],
        out_specs=pl.BlockSpec((t, W_SB), qmap),
        out_shape=jax.ShapeDtypeStruct((batch * seq, W_SB), BF16),
        scratch_shapes=[pltpu.VMEM((t, W_SB), F32)],
        compiler_params=pltpu.CompilerParams(
            dimension_semantics=("arbitrary", "arbitrary"), vmem_limit_bytes=VMEM_LIMIT),
        name="stickbreak",
    )(q, k, v)


def _dsa_tile(width, i, dq_ref, iq_ref, iw_ref, dk_ref, dv_ref, ik_ref, o_ref, score_sc, bias_sc):
    tq = DSA_TQ
    lane = lax.broadcasted_iota(jnp.int32, (1, LANES), 1)
    half = lane // HEAD_DIM
    t_idx = i * tq + lax.broadcasted_iota(jnp.int32, (tq, 1), 0)

    iq = iq_ref[...]
    zero_q = jnp.zeros((tq, LANES), BF16)
    qstack = jnp.concatenate(
        [jnp.where(half == (h % 2), iq[:, (h // 2) * LANES:(h // 2 + 1) * LANES], zero_q)
         for h in range(IDX_HEADS)], axis=0)
    iw = iw_ref[...]
    wcol = [iw[:, h:h + 1] for h in range(IDX_HEADS)]
    lo = hi = None
    for c in range(width // DSA_KCHUNK):
        ksl = slice(c * DSA_KCHUNK, (c + 1) * DSA_KCHUNK)
        dots = _dot_t(qstack, ik_ref[ksl, :])
        score = jnp.zeros((tq, DSA_KCHUNK), F32)
        for h in range(IDX_HEADS):
            score = score + jnp.maximum(dots[h * tq:(h + 1) * tq], 0.0) * wcol[h]
        kpos = c * DSA_KCHUNK + lax.broadcasted_iota(jnp.int32, (1, DSA_KCHUNK), 1)
        causal = kpos <= t_idx
        score_sc[:, ksl] = jnp.where(causal, score, -jnp.inf)
        cmin = jnp.min(jnp.where(causal, score, jnp.inf), axis=1, keepdims=True)
        cmax = jnp.max(jnp.where(causal, score, -jnp.inf), axis=1, keepdims=True)
        lo = cmin if c == 0 else jnp.minimum(lo, cmin)
        hi = cmax if c == 0 else jnp.maximum(hi, cmax)

    k_row = jnp.minimum(t_idx + 1, TOPK_MAX).astype(F32)
    hi = hi + (jnp.abs(hi) + 1.0)

    def count(pred):
        return jnp.sum(jnp.where(pred, 1.0, 0.0), axis=1, keepdims=True)

    def bisect(it, bracket):
        lo, hi = bracket
        mid = lo + 0.5 * (hi - lo)
        ge = count(score_sc[:, :width] >= mid) >= k_row
        return jnp.where(ge, mid, lo), jnp.where(ge, hi, mid)

    lo, hi = lax.fori_loop(0, DSA_BISECT_STEPS, bisect, (lo, hi), unroll=DSA_BISECT_UNROLL)

    surplus = jnp.max(count(score_sc[:, :width] >= lo) - k_row)

    @pl.when(surplus <= 0.0)
    def _():
        bias_sc[:, :width] = jnp.where(score_sc[:, :width] >= lo, 0.0, NEG_BIG)

    @pl.when(surplus > 0.0)
    def _():
        def unresolved(lo, hi):
            sc = score_sc[:, :width]
            inside = jnp.where(sc >= lo, jnp.where(sc < hi, 1.0, 0.0), 0.0) > 0.0
            top = jnp.max(jnp.where(inside, sc, -jnp.inf), axis=1, keepdims=True)
            bot = jnp.min(jnp.where(inside, sc, jnp.inf), axis=1, keepdims=True)
            extra = count(sc >= lo) > k_row
            return jnp.max(jnp.where(extra, jnp.where(top > bot, 1.0, 0.0), 0.0))

        def more(state):
            trip, pending, _, _ = state
            return jnp.logical_and(pending > 0.0, trip < DSA_BISECT_MAX_TRIPS)

        def refine(state):
            trip, _, lo, hi = state
            for _ in range(DSA_BISECT_UNROLL):
                lo, hi = bisect(0, (lo, hi))
            return trip + 1, unresolved(lo, hi), lo, hi

        _, _, lo_f, hi_f = lax.while_loop(more, refine, (jnp.int32(0), unresolved(lo, hi), lo, hi))

        need = k_row - count(score_sc[:, :width] >= hi_f)
        r128 = lax.broadcasted_iota(jnp.int32, (LANES, LANES), 0)
        c128 = lax.broadcasted_iota(jnp.int32, (LANES, LANES), 1)
        earlier = jnp.where(r128 < c128, 1.0, 0.0).astype(BF16)
        ties_before = jnp.zeros((tq, 1), F32)
        for c in range(width // LANES):
            ksl = slice(c * LANES, (c + 1) * LANES)
            sc = score_sc[:, ksl]
            tied = jnp.where(sc >= lo_f, jnp.where(sc < hi_f, 1.0, 0.0), 0.0)
            rank = _dot(tied.astype(BF16), earlier) + ties_before
            tie_bias = jnp.where(rank < need, 0.0, NEG_BIG)
            bias_sc[:, ksl] = jnp.where(sc >= hi_f, 0.0, jnp.where(sc >= lo_f, tie_bias, NEG_BIG))
            ties_before = ties_before + jnp.sum(tied, axis=1, keepdims=True)

    dq = dq_ref[...]
    heads_per_group = DSA_HEADS // DSA_KV_HEADS
    outs = []
    for g in range(DSA_KV_HEADS):
        qg = jnp.concatenate(
            [jnp.where(half == g, dq[:, p * LANES:(p + 1) * LANES], zero_q)
             for p in range(heads_per_group)], axis=0)
        m = denom = acc = None
        for c in range(width // DSA_KCHUNK):
            ksl = slice(c * DSA_KCHUNK, (c + 1) * DSA_KCHUNK)
            logits = _dot_t(qg, dk_ref[ksl, :])
            logits = jnp.concatenate(
                [logits[p * tq:(p + 1) * tq] + bias_sc[:, ksl] for p in range(heads_per_group)], axis=0)
            cmax = jnp.max(logits, axis=1, keepdims=True)
            if c == 0:
                m = cmax
                pexp = jnp.exp(logits - m)
                denom = jnp.sum(pexp, axis=1, keepdims=True)
                acc = _dot(pexp.astype(BF16), dv_ref[ksl, :])
            else:
                m_new = jnp.maximum(m, cmax)
                alpha = jnp.exp(m - m_new)
                pexp = jnp.exp(logits - m_new)
                denom = alpha * denom + jnp.sum(pexp, axis=1, keepdims=True)
                acc = alpha * acc + _dot(pexp.astype(BF16), dv_ref[ksl, :])
                m = m_new
        outs.append(acc / denom)
    for p in range(heads_per_group):
        rows = slice(p * tq, (p + 1) * tq)
        o_ref[:, p * LANES:(p + 1) * LANES] = jnp.where(half == 0, outs[0][rows], outs[1][rows]).astype(BF16)


def _dsa_kernel(dq_ref, iq_ref, iw_ref, dk_ref, dv_ref, ik_ref, o_ref, score_sc, bias_sc, *, seq):
    i = pl.program_id(1)
    tiles_per_width = DSA_KCHUNK // DSA_TQ
    for v in range(seq // DSA_KCHUNK):
        @pl.when(i // tiles_per_width == v)
        def _(v=v):
            _dsa_tile((v + 1) * DSA_KCHUNK, i, dq_ref, iq_ref, iw_ref, dk_ref, dv_ref, ik_ref,
                      o_ref, score_sc, bias_sc)


def _dsa_call(dq, iq, iw, dk, dv, ik, batch, seq):
    tq = DSA_TQ
    nq = seq // tq
    qmap = lambda b, i: (b * nq + i, 0)
    kvmap = lambda b, i: (b, 0)
    return pl.pallas_call(
        functools.partial(_dsa_kernel, seq=seq),
        grid=(batch, nq),
        in_specs=[
            pl.BlockSpec((tq, 512), qmap),
            pl.BlockSpec((tq, 512), qmap),
            pl.BlockSpec((tq, LANES), qmap),
            pl.BlockSpec((seq, LANES), kvmap),
            pl.BlockSpec((seq, LANES), kvmap),
            pl.BlockSpec((seq, LANES), kvmap),
        ],
        out_specs=pl.BlockSpec((tq, 512), qmap),
        out_shape=jax.ShapeDtypeStruct((batch * seq, 512), BF16),
        scratch_shapes=[pltpu.VMEM((tq, seq), F32), pltpu.VMEM((tq, seq), F32)],
        compiler_params=pltpu.CompilerParams(
            dimension_semantics=("arbitrary", "arbitrary"), vmem_limit_bytes=VMEM_LIMIT),
        name="dsa",
    )(dq, iq, iw, dk, dv, ik)


def _memkv_kernel(m_ref, g_ref, w_ref, k_ref, v_ref):
    mb = _rms(m_ref[...], g_ref[...]).astype(BF16)
    kv = _dot(mb, w_ref[...])
    half = MEM_HEADS * MEM_HEAD_DIM
    k_ref[...] = kv[:, :half].astype(BF16)
    v_ref[...] = kv[:, half:].astype(BF16)


def _memkv_call(mem2, g, w_ckv, batch):
    half = MEM_HEADS * MEM_HEAD_DIM
    row = lambda b: (b, 0)
    const = lambda b: (0, 0)
    return pl.pallas_call(
        _memkv_kernel,
        grid=(batch,),
        in_specs=[
            pl.BlockSpec((N_MEM, D_MODEL), row),
            pl.BlockSpec((1, D_MODEL), const),
            pl.BlockSpec((D_MODEL, 2 * half), const),
        ],
        out_specs=[pl.BlockSpec((N_MEM, half), row)] * 2,
        out_shape=[jax.ShapeDtypeStruct((batch * N_MEM, half), BF16)] * 2,
        compiler_params=pltpu.CompilerParams(
            dimension_semantics=("arbitrary",), vmem_limit_bytes=VMEM_LIMIT),
        name="memkv",
    )(mem2, g, w_ckv)


def _mix_kernel(x_ref, osb_ref, odsa_ref, g1_ref, wg_ref, bg_ref, wbs_ref, wbd_ref, wo_ref,
                g2_ref, wcq_ref, km_ref, vm_ref, wco_ref, h_ref):
    x = x_ref[...]
    ub = _rms(x, g1_ref[...]).astype(BF16)
    gates = 1.0 / (1.0 + jnp.exp(-(_dot(ub, wg_ref[...]) + bg_ref[...])))
    merged = (gates[:, :D_MODEL] * _dot(osb_ref[...], wbs_ref[...])
              + gates[:, D_MODEL:] * _dot(odsa_ref[...], wbd_ref[...]))
    h1 = x + _dot(merged.astype(BF16), wo_ref[...])

    u2 = _rms(h1, g2_ref[...]).astype(BF16)
    qb = _dot(u2, wcq_ref[...]).astype(BF16)
    km = km_ref[...]
    vm = vm_ref[...]
    outs = []
    for h in range(MEM_HEADS):
        sl = slice(h * MEM_HEAD_DIM, (h + 1) * MEM_HEAD_DIM)
        logits = _dot_t(qb[:, sl], km[:, sl]) * (MEM_HEAD_DIM ** -0.5)
        m = jnp.max(logits, axis=1, keepdims=True)
        pexp = jnp.exp(logits - m)
        denom = jnp.sum(pexp, axis=1, keepdims=True)
        outs.append((_dot(pexp.astype(BF16), vm[:, sl]) / denom).astype(BF16))
    o = jnp.concatenate(outs, axis=1)
    h_ref[...] = h1 + _dot(o, wco_ref[...])


def _mix_call(x2, osb, odsa, g1, wg, bg, wbs, wbd, wo, g2, wcq, km, vm, wco, seq):
    n = x2.shape[0]
    tm = ROW_TILE
    tiles_per_seq = seq // tm
    half = MEM_HEADS * MEM_HEAD_DIM
    row = lambda i: (i, 0)
    const = lambda i: (0, 0)
    bat = lambda i: (i // tiles_per_seq, 0)

    def full(a):
        return pl.BlockSpec(a.shape, const)

    return pl.pallas_call(
        _mix_kernel,
        grid=(n // tm,),
        in_specs=[
            pl.BlockSpec((tm, D_MODEL), row),
            pl.BlockSpec((tm, 512), row),
            pl.BlockSpec((tm, 512), row),
            full(g1), full(wg), full(bg), full(wbs), full(wbd), full(wo),
            full(g2), full(wcq),
            pl.BlockSpec((N_MEM, half), bat),
            pl.BlockSpec((N_MEM, half), bat),
            full(wco),
        ],
        out_specs=pl.BlockSpec((tm, D_MODEL), row),
        out_shape=jax.ShapeDtypeStruct((n, D_MODEL), F32),
        compiler_params=pltpu.CompilerParams(
            dimension_semantics=("arbitrary",), vmem_limit_bytes=VMEM_LIMIT),
        name="mix",
    )(x2, osb, odsa, g1, wg, bg, wbs, wbd, wo, g2, wcq, km, vm, wco)


def _mlp_kernel(h_ref, g_ref, wu_ref, wd_ref, gf_ref, o_ref, *, final_norm):
    h = h_ref[...]
    ub = _rms(h, g_ref[...]).astype(BF16)
    acc = h
    for c in range(D_FF // D_MODEL):
        sl = slice(c * D_MODEL, (c + 1) * D_MODEL)
        hid = jnp.maximum(_dot(ub, wu_ref[:, sl]), 0.0)
        acc = acc + _dot((hid * hid).astype(BF16), wd_ref[sl, :])
    if final_norm:
        acc = _rms(acc, gf_ref[...])
    o_ref[...] = acc


def _mlp_call(h2, g, wu, wd, gf, final_norm):
    n = h2.shape[0]
    tm = ROW_TILE
    row = lambda i: (i, 0)
    const = lambda i: (0, 0)
    return pl.pallas_call(
        functools.partial(_mlp_kernel, final_norm=final_norm),
        grid=(n // tm,),
        in_specs=[
            pl.BlockSpec((tm, D_MODEL), row),
            pl.BlockSpec((1, D_MODEL), const),
            pl.BlockSpec((D_MODEL, D_FF), const),
            pl.BlockSpec((D_FF, D_MODEL), const),
            pl.BlockSpec((1, D_MODEL), const),
        ],
        out_specs=pl.BlockSpec((tm, D_MODEL), row),
        out_shape=jax.ShapeDtypeStruct((n, D_MODEL), F32),
        compiler_params=pltpu.CompilerParams(
            dimension_semantics=("arbitrary",), vmem_limit_bytes=VMEM_LIMIT),
        name="mlp",
    )(h2, g, wu, wd, gf)


def _pack_w_in(w_in):
    offs = np.cumsum([0, 512, 512, 512, 512, 128, 128, 512, 64, 8])
    sbq, sbk, sbv, dq, dk, dv, iq, ik, iw = [w_in[:, offs[j]:offs[j + 1]] for j in range(9)]
    dq = dq.reshape(D_MODEL, 2, 4, HEAD_DIM).transpose(0, 2, 1, 3).reshape(D_MODEL, 512)
    iw = jnp.pad(iw, ((0, 0), (0, LANES - IDX_HEADS)))
    return jnp.concatenate([sbq, sbk, sbv, dq, dk, dv, iq, ik, ik, iw], axis=1).astype(BF16)


def _rope_tables(seq):
    inv = ROPE_THETA ** (-jnp.arange(0, HEAD_DIM, 2, dtype=F32) / HEAD_DIM)
    ang = jnp.arange(seq).astype(F32)[:, None] * inv[None, :]
    cos = jnp.cos(ang)
    sin = jnp.sin(ang)
    cos_t = jnp.tile(cos, (1, LANES // HALF_ROT))
    sin_t = jnp.tile(jnp.concatenate([-sin, sin], axis=1), (1, LANES // HEAD_DIM))
    return cos_t, sin_t


def kernel(x, mem, norm_mix, w_in, w_branch_sb, w_branch_dsa, w_gate, b_gate, w_out, norm_cross,
           norm_mem, w_cq, w_ckv, w_co, norm_mlp, w_up, w_down, norm_final):
    batch, seq, d = x.shape
    depth = w_in.shape[0]
    cos_t, sin_t = _rope_tables(seq)
    h = x.reshape(batch * seq, d)
    mem2 = mem.reshape(batch * mem.shape[1], d)
    gf = norm_final.reshape(1, d)
    for l in range(depth):
        w_packed = _pack_w_in(w_in[l])
        wbd = (w_branch_dsa[l].reshape(2, 4, HEAD_DIM, d).transpose(1, 0, 2, 3)
               .reshape(DSA_HEADS * HEAD_DIM, d).astype(BF16))
        sbq, sbk, sbv, dq, dk, dv, iq, ik, iw = _proj_call(
            h, norm_mix[l].reshape(1, d), w_packed, cos_t, sin_t, seq)
        o_sb = _sb_call(sbq, sbk, sbv, batch, seq)
        o_dsa = _dsa_call(dq, iq, iw, dk, dv, ik, batch, seq)
        km, vm = _memkv_call(mem2, norm_mem[l].reshape(1, d), w_ckv[l].astype(BF16), batch)
        h2 = _mix_call(h, o_sb, o_dsa, norm_mix[l].reshape(1, d), w_gate[l].astype(BF16),
                       b_gate[l].reshape(1, -1), w_branch_sb[l].astype(BF16), wbd,
                       w_out[l].astype(BF16), norm_cross[l].reshape(1, d), w_cq[l].astype(BF16),
                       km, vm, w_co[l].astype(BF16), seq)
        h = _mlp_call(h2, norm_mlp[l].reshape(1, d), w_up[l].astype(BF16), w_down[l].astype(BF16),
                      gf, final_norm=(l == depth - 1))
    return h.reshape(batch, seq, d)
```

```python
import functools

import jax
import jax.numpy as jnp
import numpy as np
from jax import lax
from jax.experimental import pallas as pl
from jax.experimental.pallas import tpu as pltpu

F32 = jnp.float32
BF16 = jnp.bfloat16

D_MODEL = 1024
N_MEM = 256
SB_HEADS = 8
DSA_HEADS = 8
DSA_KV_HEADS = 2
IDX_HEADS = 8
HEAD_DIM = 64
TOPK_MAX = 256
MEM_HEADS = 4
MEM_HEAD_DIM = 128
D_FF = 4 * D_MODEL
ROPE_THETA = 10000.0
EPS = 1e-6

LANES = 128
HALF_ROT = HEAD_DIM // 2

W_SB = SB_HEADS * HEAD_DIM
OFF_SBQ = 0
OFF_SBK = OFF_SBQ + W_SB
OFF_SBV = OFF_SBK + W_SB
OFF_DQ = OFF_SBV + W_SB
OFF_DK = OFF_DQ + DSA_HEADS * HEAD_DIM
OFF_DV = OFF_DK + DSA_KV_HEADS * HEAD_DIM
OFF_IQ = OFF_DV + DSA_KV_HEADS * HEAD_DIM
OFF_IK = OFF_IQ + IDX_HEADS * HEAD_DIM
OFF_IW = OFF_IK + LANES
D_IN_PACKED = OFF_IW + LANES

ROW_TILE = 512
SB_TILE = 256
DSA_TQ = 256
DSA_KCHUNK = 512
VMEM_LIMIT = 56 * 1024 * 1024
NEG_BIG = -1e30
DSA_BISECT_STEPS = 32
DSA_BISECT_UNROLL = 4
LOG2_E = 1.4426950408889634


def _rms(x, g):
    return x * lax.rsqrt(jnp.mean(x * x, axis=-1, keepdims=True) + EPS) * g


def _dot(a, b):
    return jnp.dot(a, b, preferred_element_type=F32)


def _dot_t(a, b):
    return lax.dot_general(a, b, (((1,), (1,)), ((), ())), preferred_element_type=F32)


def _proj_kernel(x_ref, g_ref, w_ref, cos_ref, sin_ref,
                 sbq_ref, sbk_ref, sbv_ref, dq_ref, dk_ref, dv_ref, iq_ref, ik_ref, iw_ref):
    ub = _rms(x_ref[...], g_ref[...]).astype(BF16)
    cos = cos_ref[...]
    sin = sin_ref[...]
    lane = lax.broadcasted_iota(jnp.int32, (1, LANES), 1)
    first_half = (lane % HEAD_DIM) < HALF_ROT

    def mm(off, width=LANES):
        return _dot(ub, w_ref[:, off:off + width])

    def rope(s):
        rot = jnp.where(first_half, pltpu.roll(s, LANES - HALF_ROT, 1), pltpu.roll(s, HALF_ROT, 1))
        return s * cos + rot * sin

    q_scale = HEAD_DIM ** -0.5
    sbq_ref[...] = (mm(OFF_SBQ, W_SB) * (q_scale * LOG2_E)).astype(BF16)
    sbk_ref[...] = mm(OFF_SBK, W_SB).astype(BF16)
    sbv_ref[...] = mm(OFF_SBV, W_SB).astype(BF16)
    dq = mm(OFF_DQ, 4 * LANES)
    iq = mm(OFF_IQ, 4 * LANES)
    for p in range(4):
        sl = slice(p * LANES, (p + 1) * LANES)
        dq_ref[:, sl] = (rope(dq[:, sl]) * q_scale).astype(BF16)
        iq_ref[:, sl] = (rope(iq[:, sl]) * q_scale).astype(BF16)
    dkv = mm(OFF_DK, 2 * LANES)
    dk_ref[...] = rope(dkv[:, :LANES]).astype(BF16)
    dv_ref[...] = dkv[:, LANES:].astype(BF16)
    ikw = mm(OFF_IK, 2 * LANES)
    ik_ref[...] = rope(ikw[:, :LANES]).astype(BF16)
    iw_ref[...] = ikw[:, LANES:] * (IDX_HEADS ** -0.5)


def _proj_call(x2, g, w_packed, cos_t, sin_t, seq):
    n = x2.shape[0]
    tm = ROW_TILE
    tiles_per_seq = seq // tm
    row = lambda i: (i, 0)
    const = lambda i: (0, 0)
    pos = lambda i: (i % tiles_per_seq, 0)
    out_w = [W_SB, W_SB, W_SB, 512, LANES, LANES, 512, LANES, LANES]
    out_dt = [BF16] * 8 + [F32]
    return pl.pallas_call(
        _proj_kernel,
        grid=(n // tm,),
        in_specs=[
            pl.BlockSpec((tm, D_MODEL), row),
            pl.BlockSpec((1, D_MODEL), const),
            pl.BlockSpec((D_MODEL, D_IN_PACKED), const),
            pl.BlockSpec((tm, LANES), pos),
            pl.BlockSpec((tm, LANES), pos),
        ],
        out_specs=[pl.BlockSpec((tm, w), row) for w in out_w],
        out_shape=[jax.ShapeDtypeStruct((n, w), dt) for w, dt in zip(out_w, out_dt)],
        compiler_params=pltpu.CompilerParams(
            dimension_semantics=("arbitrary",), vmem_limit_bytes=VMEM_LIMIT),
        name="proj",
    )(x2, g, w_packed, cos_t, sin_t)


def _sb_kernel(q_ref, k_ref, v_ref, o_ref, acc_sc):
    t = SB_TILE
    i = pl.program_id(1)
    lane = lax.broadcasted_iota(jnp.int32, (1, LANES), 1)
    first_head = lane < HEAD_DIM
    row = lax.broadcasted_iota(jnp.int32, (t, t), 0)
    col = lax.broadcasted_iota(jnp.int32, (t, t), 1)
    strict = col < row
    later = jnp.where(row > col, 1.0, 0.0).astype(BF16)
    zero_q = jnp.zeros((t, LANES), BF16)
    qh = []
    for p in range(SB_HEADS // 2):
        slab = q_ref[:, p * LANES:(p + 1) * LANES]
        qh.append(jnp.where(first_head, slab, zero_q))
        qh.append(jnp.where(first_head, zero_q, slab))

    def tile(start, diag, carries):
        new = []
        for p in range(SB_HEADS // 2):
            sl = slice(p * LANES, (p + 1) * LANES)
            kb = k_ref[pl.ds(start, t), sl]
            vb = v_ref[pl.ds(start, t), sl]
            av = []
            for h in range(2):
                z = _dot_t(qh[2 * p + h], kb)
                z_neg = jnp.minimum(z, 0.0)
                z_pos = z - z_neg
                sp = jnp.log2(1.0 + jnp.exp2(z_neg - z_pos))
                log_beta = z_neg - sp
                neg_log_1mb = z_pos + sp
                if diag:
                    neg_log_1mb = jnp.where(strict, neg_log_1mb, 0.0)
                within = _dot(neg_log_1mb.astype(BF16), later)
                a = jnp.exp2(log_beta - (within + carries[2 * p + h]))
                if diag:
                    a = jnp.where(strict, a, 0.0)
                av.append(_dot(a.astype(BF16), vb))
                new.append(carries[2 * p + h] + (within[:, 0:1] + neg_log_1mb[:, 0:1]))
            contrib = jnp.where(first_head, av[0], av[1])
            if diag:
                acc_sc[:, sl] = contrib
            else:
                acc_sc[:, sl] += contrib
        return tuple(new)

    zc = jnp.zeros((t, 1), F32)
    carries = tile(pl.multiple_of(i * t, t), True, (zc,) * SB_HEADS)

    def body(s, carries):
        return tile(pl.multiple_of((i - 1 - s) * t, t), False, carries)

    lax.fori_loop(0, i, body, carries)
    o_ref[...] = acc_sc[...].astype(BF16)


def _sb_call(q, k, v, batch, seq):
    t = SB_TILE
    nq = seq // t
    qmap = lambda b, i: (b * nq + i, 0)
    kvmap = lambda b, i: (b, 0)
    return pl.pallas_call(
        _sb_kernel,
        grid=(batch, nq),
        in_specs=[
            pl.BlockSpec((t, W_SB), qmap),
            pl.BlockSpec((seq, W_SB), kvmap),
            pl.BlockSpec((seq, W_SB), kvmap),
        ],
        out_specs=pl.BlockSpec((t, W_SB), qmap),
        out_shape=jax.ShapeDtypeStruct((batch * seq, W_SB), BF16),
        scratch_shapes=[pltpu.VMEM((t, W_SB), F32)],
        compiler_params=pltpu.CompilerParams(
            dimension_semantics=("arbitrary", "arbitrary"), vmem_limit_bytes=VMEM_LIMIT),
        name="stickbreak",
    )(q, k, v)


def _dsa_tile(width, i, dq_ref, iq_ref, iw_ref, dk_ref, dv_ref, ik_ref, o_ref, score_sc, bias_sc):
    tq = DSA_TQ
    lane = lax.broadcasted_iota(jnp.int32, (1, LANES), 1)
    half = lane // HEAD_DIM
    t_idx = i * tq + lax.broadcasted_iota(jnp.int32, (tq, 1), 0)

    iq = iq_ref[...]
    zero_q = jnp.zeros((tq, LANES), BF16)
    qstack = jnp.concatenate(
        [jnp.where(half == (h % 2), iq[:, (h // 2) * LANES:(h // 2 + 1) * LANES], zero_q)
         for h in range(IDX_HEADS)], axis=0)
    iw = iw_ref[...]
    wcol = [iw[:, h:h + 1] for h in range(IDX_HEADS)]
    lo = hi = None
    for c in range(width // DSA_KCHUNK):
        ksl = slice(c * DSA_KCHUNK, (c + 1) * DSA_KCHUNK)
        dots = _dot_t(qstack, ik_ref[ksl, :])
        score = jnp.zeros((tq, DSA_KCHUNK), F32)
        for h in range(IDX_HEADS):
            score = score + jnp.maximum(dots[h * tq:(h + 1) * tq], 0.0) * wcol[h]
        kpos = c * DSA_KCHUNK + lax.broadcasted_iota(jnp.int32, (1, DSA_KCHUNK), 1)
        causal = kpos <= t_idx
        score_sc[:, ksl] = jnp.where(causal, score, -jnp.inf)
        cmin = jnp.min(jnp.where(causal, score, jnp.inf), axis=1, keepdims=True)
        cmax = jnp.max(jnp.where(causal, score, -jnp.inf), axis=1, keepdims=True)
        lo = cmin if c == 0 else jnp.minimum(lo, cmin)
        hi = cmax if c == 0 else jnp.maximum(hi, cmax)

    k_row = jnp.minimum(t_idx + 1, TOPK_MAX).astype(F32)
    hi = hi + (jnp.abs(hi) + 1.0)

    def count(pred):
        return jnp.sum(jnp.where(pred, 1.0, 0.0), axis=1, keepdims=True)

    def bisect(it, bracket):
        lo, hi = bracket
        mid = lo + 0.5 * (hi - lo)
        ge = count(score_sc[:, :width] >= mid) >= k_row
        return jnp.where(ge, mid, lo), jnp.where(ge, hi, mid)

    lo, hi = lax.fori_loop(0, DSA_BISECT_STEPS, bisect, (lo, hi), unroll=DSA_BISECT_UNROLL)

    surplus = jnp.max(count(score_sc[:, :width] >= lo) - k_row)

    @pl.when(surplus <= 0.0)
    def _():
        bias_sc[:, :width] = jnp.where(score_sc[:, :width] >= lo, 0.0, NEG_BIG)

    @pl.when(surplus > 0.0)
    def _():
        lo_f, hi_f = lo, hi
        need = k_row - count(score_sc[:, :width] >= hi_f)
        r128 = lax.broadcasted_iota(jnp.int32, (LANES, LANES), 0)
        c128 = lax.broadcasted_iota(jnp.int32, (LANES, LANES), 1)
        earlier = jnp.where(r128 < c128, 1.0, 0.0).astype(BF16)
        ties_before = jnp.zeros((tq, 1), F32)
        for c in range(width // LANES):
            ksl = slice(c * LANES, (c + 1) * LANES)
            sc = score_sc[:, ksl]
            tied = jnp.where(sc >= lo_f, jnp.where(sc < hi_f, 1.0, 0.0), 0.0)
            rank = _dot(tied.astype(BF16), earlier) + ties_before
            tie_bias = jnp.where(rank < need, 0.0, NEG_BIG)
            bias_sc[:, ksl] = jnp.where(sc >= hi_f, 0.0, jnp.where(sc >= lo_f, tie_bias, NEG_BIG))
            ties_before = ties_before + jnp.sum(tied, axis=1, keepdims=True)

    dq = dq_ref[...]
    heads_per_group = DSA_HEADS // DSA_KV_HEADS
    outs = []
    for g in range(DSA_KV_HEADS):
        qg = jnp.concatenate(
            [jnp.where(half == g, dq[:, p * LANES:(p + 1) * LANES], zero_q)
             for p in range(heads_per_group)], axis=0)
        m = denom = acc = None
        for c in range(width // DSA_KCHUNK):
            ksl = slice(c * DSA_KCHUNK, (c + 1) * DSA_KCHUNK)
            logits = _dot_t(qg, dk_ref[ksl, :])
            logits = jnp.concatenate(
                [logits[p * tq:(p + 1) * tq] + bias_sc[:, ksl] for p in range(heads_per_group)], axis=0)
            cmax = jnp.max(logits, axis=1, keepdims=True)
            if c == 0:
                m = cmax
                pexp = jnp.exp(logits - m)
                denom = jnp.sum(pexp, axis=1, keepdims=True)
                acc = _dot(pexp.astype(BF16), dv_ref[ksl, :])
            else:
                m_new = jnp.maximum(m, cmax)
                alpha = jnp.exp(m - m_new)
                pexp = jnp.exp(logits - m_new)
                denom = alpha * denom + jnp.sum(pexp, axis=1, keepdims=True)
                acc = alpha * acc + _dot(pexp.astype(BF16), dv_ref[ksl, :])
                m = m_new
        outs.append(acc / denom)
    for p in range(heads_per_group):
        rows = slice(p * tq, (p + 1) * tq)
        o_ref[:, p * LANES:(p + 1) * LANES] = jnp.where(half == 0, outs[0][rows], outs[1][rows]).astype(BF16)


def _dsa_kernel(dq_ref, iq_ref, iw_ref, dk_ref, dv_ref, ik_ref, o_ref, score_sc, bias_sc, *, seq):
    i = pl.program_id(1)
    tiles_per_width = DSA_KCHUNK // DSA_TQ
    for v in range(seq // DSA_KCHUNK):
        @pl.when(i // tiles_per_width == v)
        def _(v=v):
            _dsa_tile((v + 1) * DSA_KCHUNK, i, dq_ref, iq_ref, iw_ref, dk_ref, dv_ref, ik_ref,
                      o_ref, score_sc, bias_sc)


def _dsa_call(dq, iq, iw, dk, dv, ik, batch, seq):
    tq = DSA_TQ
    nq = seq // tq
    qmap = lambda b, i: (b * nq + i, 0)
    kvmap = lambda b, i: (b, 0)
    return pl.pallas_call(
        functools.partial(_dsa_kernel, seq=seq),
        grid=(batch, nq),
        in_specs=[
            pl.BlockSpec((tq, 512), qmap),
            pl.BlockSpec((tq, 512), qmap),
            pl.BlockSpec((tq, LANES), qmap),
            pl.BlockSpec((seq, LANES), kvmap),
            pl.BlockSpec((seq, LANES), kvmap),
            pl.BlockSpec((seq, LANES), kvmap),
        ],
        out_specs=pl.BlockSpec((tq, 512), qmap),
        out_shape=jax.ShapeDtypeStruct((batch * seq, 512), BF16),
        scratch_shapes=[pltpu.VMEM((tq, seq), F32), pltpu.VMEM((tq, seq), F32)],
        compiler_params=pltpu.CompilerParams(
            dimension_semantics=("arbitrary", "arbitrary"), vmem_limit_bytes=VMEM_LIMIT),
        name="dsa",
    )(dq, iq, iw, dk, dv, ik)


def _memkv_kernel(m_ref, g_ref, w_ref, k_ref, v_ref):
    mb = _rms(m_ref[...], g_ref[...]).astype(BF16)
    kv = _dot(mb, w_ref[...])
    half = MEM_HEADS * MEM_HEAD_DIM
    k_ref[...] = kv[:, :half].astype(BF16)
    v_ref[...] = kv[:, half:].astype(BF16)


def _memkv_call(mem2, g, w_ckv, batch):
    half = MEM_HEADS * MEM_HEAD_DIM
    row = lambda b: (b, 0)
    const = lambda b: (0, 0)
    return pl.pallas_call(
        _memkv_kernel,
        grid=(batch,),
        in_specs=[
            pl.BlockSpec((N_MEM, D_MODEL), row),
            pl.BlockSpec((1, D_MODEL), const),
            pl.BlockSpec((D_MODEL, 2 * half), const),
        ],
        out_specs=[pl.BlockSpec((N_MEM, half), row)] * 2,
        out_shape=[jax.ShapeDtypeStruct((batch * N_MEM, half), BF16)] * 2,
        compiler_params=pltpu.CompilerParams(
            dimension_semantics=("arbitrary",), vmem_limit_bytes=VMEM_LIMIT),
        name="memkv",
    )(mem2, g, w_ckv)


def _mix_kernel(x_ref, osb_ref, odsa_ref, g1_ref, wg_ref, bg_ref, wbs_ref, wbd_ref, wo_ref,
                g2_ref, wcq_ref, km_ref, vm_ref, wco_ref, h_ref):
    x = x_ref[...]
    ub = _rms(x, g1_ref[...]).astype(BF16)
    gates = 1.0 / (1.0 + jnp.exp(-(_dot(ub, wg_ref[...]) + bg_ref[...])))
    merged = (gates[:, :D_MODEL] * _dot(osb_ref[...], wbs_ref[...])
              + gates[:, D_MODEL:] * _dot(odsa_ref[...], wbd_ref[...]))
    h1 = x + _dot(merged.astype(BF16), wo_ref[...])

    u2 = _rms(h1, g2_ref[...]).astype(BF16)
    qb = _dot(u2, wcq_ref[...]).astype(BF16)
    km = km_ref[...]
    vm = vm_ref[...]
    outs = []
    for h in range(MEM_HEADS):
        sl = slice(h * MEM_HEAD_DIM, (h + 1) * MEM_HEAD_DIM)
        logits = _dot_t(qb[:, sl], km[:, sl]) * (MEM_HEAD_DIM ** -0.5)
        m = jnp.max(logits, axis=1, keepdims=True)
        pexp = jnp.exp(logits - m)
        denom = jnp.sum(pexp, axis=1, keepdims=True)
        outs.append((_dot(pexp.astype(BF16), vm[:, sl]) / denom).astype(BF16))
    o = jnp.concatenate(outs, axis=1)
    h_ref[...] = h1 + _dot(o, wco_ref[...])


def _mix_call(x2, osb, odsa, g1, wg, bg, wbs, wbd, wo, g2, wcq, km, vm, wco, seq):
    n = x2.shape[0]
    tm = ROW_TILE
    tiles_per_seq = seq // tm
    half = MEM_HEADS * MEM_HEAD_DIM
    row = lambda i: (i, 0)
    const = lambda i: (0, 0)
    bat = lambda i: (i // tiles_per_seq, 0)

    def full(a):
        return pl.BlockSpec(a.shape, const)

    return pl.pallas_call(
        _mix_kernel,
        grid=(n // tm,),
        in_specs=[
            pl.BlockSpec((tm, D_MODEL), row),
            pl.BlockSpec((tm, 512), row),
            pl.BlockSpec((tm, 512), row),
            full(g1), full(wg), full(bg), full(wbs), full(wbd), full(wo),
            full(g2), full(wcq),
            pl.BlockSpec((N_MEM, half), bat),
            pl.BlockSpec((N_MEM, half), bat),
            full(wco),
        ],
        out_specs=pl.BlockSpec((tm, D_MODEL), row),
        out_shape=jax.ShapeDtypeStruct((n, D_MODEL), F32),
        compiler_params=pltpu.CompilerParams(
            dimension_semantics=("arbitrary",), vmem_limit_bytes=VMEM_LIMIT),
        name="mix",
    )(x2, osb, odsa, g1, wg, bg, wbs, wbd, wo, g2, wcq, km, vm, wco)


def _mlp_kernel(h_ref, g_ref, wu_ref, wd_ref, gf_ref, o_ref, *, final_norm):
    h = h_ref[...]
    ub = _rms(h, g_ref[...]).astype(BF16)
    acc = h
    for c in range(D_FF // D_MODEL):
        sl = slice(c * D_MODEL, (c + 1) * D_MODEL)
        hid = jnp.maximum(_dot(ub, wu_ref[:, sl]), 0.0)
        acc = acc + _dot((hid * hid).astype(BF16), wd_ref[sl, :])
    if final_norm:
        acc = _rms(acc, gf_ref[...])
    o_ref[...] = acc


def _mlp_call(h2, g, wu, wd, gf, final_norm):
    n = h2.shape[0]
    tm = ROW_TILE
    row = lambda i: (i, 0)
    const = lambda i: (0, 0)
    return pl.pallas_call(
        functools.partial(_mlp_kernel, final_norm=final_norm),
        grid=(n // tm,),
        in_specs=[
            pl.BlockSpec((tm, D_MODEL), row),
            pl.BlockSpec((1, D_MODEL), const),
            pl.BlockSpec((D_MODEL, D_FF), const),
            pl.BlockSpec((D_FF, D_MODEL), const),
            pl.BlockSpec((1, D_MODEL), const),
        ],
        out_specs=pl.BlockSpec((tm, D_MODEL), row),
        out_shape=jax.ShapeDtypeStruct((n, D_MODEL), F32),
        compiler_params=pltpu.CompilerParams(
            dimension_semantics=("arbitrary",), vmem_limit_bytes=VMEM_LIMIT),
        name="mlp",
    )(h2, g, wu, wd, gf)


def _pack_w_in(w_in):
    offs = np.cumsum([0, 512, 512, 512, 512, 128, 128, 512, 64, 8])
    sbq, sbk, sbv, dq, dk, dv, iq, ik, iw = [w_in[:, offs[j]:offs[j + 1]] for j in range(9)]
    dq = dq.reshape(D_MODEL, 2, 4, HEAD_DIM).transpose(0, 2, 1, 3).reshape(D_MODEL, 512)
    iw = jnp.pad(iw, ((0, 0), (0, LANES - IDX_HEADS)))
    return jnp.concatenate([sbq, sbk, sbv, dq, dk, dv, iq, ik, ik, iw], axis=1).astype(BF16)


def _rope_tables(seq):
    inv = ROPE_THETA ** (-jnp.arange(0, HEAD_DIM, 2, dtype=F32) / HEAD_DIM)
    ang = jnp.arange(seq).astype(F32)[:, None] * inv[None, :]
    cos = jnp.cos(ang)
    sin = jnp.sin(ang)
    cos_t = jnp.tile(cos, (1, LANES // HALF_ROT))
    sin_t = jnp.tile(jnp.concatenate([-sin, sin], axis=1), (1, LANES // HEAD_DIM))
    return cos_t, sin_t


def kernel(x, mem, norm_mix, w_in, w_branch_sb, w_branch_dsa, w_gate, b_gate, w_out, norm_cross,
           norm_mem, w_cq, w_ckv, w_co, norm_mlp, w_up, w_down, norm_final):
    batch, seq, d = x.shape
    depth = w_in.shape[0]
    cos_t, sin_t = _rope_tables(seq)
    h = x.reshape(batch * seq, d)
    mem2 = mem.reshape(batch * mem.shape[1], d)
    gf = norm_final.reshape(1, d)
    for l in range(depth):
        w_packed = _pack_w_in(w_in[l])
        wbd = (w_branch_dsa[l].reshape(2, 4, HEAD_DIM, d).transpose(1, 0, 2, 3)
               .reshape(DSA_HEADS * HEAD_DIM, d).astype(BF16))
        sbq, sbk, sbv, dq, dk, dv, iq, ik, iw = _proj_call(
            h, norm_mix[l].reshape(1, d), w_packed, cos_t, sin_t, seq)
        o_sb = _sb_call(sbq, sbk, sbv, batch, seq)
        o_dsa = _dsa_call(dq, iq, iw, dk, dv, ik, batch, seq)
        km, vm = _memkv_call(mem2, norm_mem[l].reshape(1, d), w_ckv[l].astype(BF16), batch)
        h2 = _mix_call(h, o_sb, o_dsa, norm_mix[l].reshape(1, d), w_gate[l].astype(BF16),
                       b_gate[l].reshape(1, -1), w_branch_sb[l].astype(BF16), wbd,
                       w_out[l].astype(BF16), norm_cross[l].reshape(1, d), w_cq[l].astype(BF16),
                       km, vm, w_co[l].astype(BF16), seq)
        h = _mlp_call(h2, norm_mlp[l].reshape(1, d), w_up[l].astype(BF16), w_down[l].astype(BF16),
                      gf, final_norm=(l == depth - 1))
    return h.reshape(batch, seq, d)
```

```python
import functools

import jax
import jax.numpy as jnp
import numpy as np
from jax import lax
from jax.experimental import pallas as pl
from jax.experimental.pallas import tpu as pltpu

F32 = jnp.float32
BF16 = jnp.bfloat16

D_MODEL = 1024
N_MEM = 256
SB_HEADS = 8
DSA_HEADS = 8
DSA_KV_HEADS = 2
IDX_HEADS = 8
HEAD_DIM = 64
TOPK_MAX = 256
MEM_HEADS = 4
MEM_HEAD_DIM = 128
D_FF = 4 * D_MODEL
ROPE_THETA = 10000.0
EPS = 1e-6

LANES = 128
HALF_ROT = HEAD_DIM // 2

W_SB = SB_HEADS * HEAD_DIM
OFF_SBQ = 0
OFF_SBK = OFF_SBQ + W_SB
OFF_SBV = OFF_SBK + W_SB
OFF_DQ = OFF_SBV + W_SB
OFF_DK = OFF_DQ + DSA_HEADS * HEAD_DIM
OFF_DV = OFF_DK + DSA_KV_HEADS * HEAD_DIM
OFF_IQ = OFF_DV + DSA_KV_HEADS * HEAD_DIM
OFF_IK = OFF_IQ + IDX_HEADS * HEAD_DIM
OFF_IW = OFF_IK + LANES
D_IN_PACKED = OFF_IW + LANES

ROW_TILE = 512
SB_TILE = 256
DSA_TQ = 256
DSA_KCHUNK = 512
VMEM_LIMIT = 56 * 1024 * 1024
NEG_BIG = -1e30
DSA_BISECT_STEPS = 32
DSA_BISECT_UNROLL = 4
LOG2_E = 1.4426950408889634


def _rms(x, g):
    return x * lax.rsqrt(jnp.mean(x * x, axis=-1, keepdims=True) + EPS) * g


def _dot(a, b):
    return jnp.dot(a, b, preferred_element_type=F32)


def _dot_t(a, b):
    return lax.dot_general(a, b, (((1,), (1,)), ((), ())), preferred_element_type=F32)


def _proj_kernel(x_ref, g_ref, w_ref, cos_ref, sin_ref,
                 sbq_ref, sbk_ref, sbv_ref, dq_ref, dk_ref, dv_ref, iq_ref, ik_ref, iw_ref):
    ub = _rms(x_ref[...], g_ref[...]).astype(BF16)
    cos = cos_ref[...]
    sin = sin_ref[...]
    lane = lax.broadcasted_iota(jnp.int32, (1, LANES), 1)
    first_half = (lane % HEAD_DIM) < HALF_ROT

    def mm(off, width=LANES):
        return _dot(ub, w_ref[:, off:off + width])

    def rope(s):
        rot = jnp.where(first_half, pltpu.roll(s, LANES - HALF_ROT, 1), pltpu.roll(s, HALF_ROT, 1))
        return s * cos + rot * sin

    q_scale = HEAD_DIM ** -0.5
    sbq_ref[...] = (mm(OFF_SBQ, W_SB) * (q_scale * LOG2_E)).astype(BF16)
    sbk_ref[...] = mm(OFF_SBK, W_SB).astype(BF16)
    sbv_ref[...] = mm(OFF_SBV, W_SB).astype(BF16)
    dq = mm(OFF_DQ, 4 * LANES)
    iq = mm(OFF_IQ, 4 * LANES)
    for p in range(4):
        sl = slice(p * LANES, (p + 1) * LANES)
        dq_ref[:, sl] = (rope(dq[:, sl]) * q_scale).astype(BF16)
        iq_ref[:, sl] = (rope(iq[:, sl]) * q_scale).astype(BF16)
    dkv = mm(OFF_DK, 2 * LANES)
    dk_ref[...] = rope(dkv[:, :LANES]).astype(BF16)
    dv_ref[...] = dkv[:, LANES:].astype(BF16)
    ikw = mm(OFF_IK, 2 * LANES)
    ik_ref[...] = rope(ikw[:, :LANES]).astype(BF16)
    iw_ref[...] = ikw[:, LANES:] * (IDX_HEADS ** -0.5)


def _proj_call(x2, g, w_packed, cos_t, sin_t, seq):
    n = x2.shape[0]
    tm = ROW_TILE
    tiles_per_seq = seq // tm
    row = lambda i: (i, 0)
    const = lambda i: (0, 0)
    pos = lambda i: (i % tiles_per_seq, 0)
    out_w = [W_SB, W_SB, W_SB, 512, LANES, LANES, 512, LANES, LANES]
    out_dt = [BF16] * 8 + [F32]
    return pl.pallas_call(
        _proj_kernel,
        grid=(n // tm,),
        in_specs=[
            pl.BlockSpec((tm, D_MODEL), row),
            pl.BlockSpec((1, D_MODEL), const),
            pl.BlockSpec((D_MODEL, D_IN_PACKED), const),
            pl.BlockSpec((tm, LANES), pos),
            pl.BlockSpec((tm, LANES), pos),
        ],
        out_specs=[pl.BlockSpec((tm, w), row) for w in out_w],
        out_shape=[jax.ShapeDtypeStruct((n, w), dt) for w, dt in zip(out_w, out_dt)],
        compiler_params=pltpu.CompilerParams(
            dimension_semantics=("arbitrary",), vmem_limit_bytes=VMEM_LIMIT),
        name="proj",
    )(x2, g, w_packed, cos_t, sin_t)


def _sb_kernel(q_ref, k_ref, v_ref, o_ref, acc_sc):
    t = SB_TILE
    i = pl.program_id(1)
    lane = lax.broadcasted_iota(jnp.int32, (1, LANES), 1)
    first_head = lane < HEAD_DIM
    row = lax.broadcasted_iota(jnp.int32, (t, t), 0)
    col = lax.broadcasted_iota(jnp.int32, (t, t), 1)
    strict = col < row
    later = jnp.where(row > col, 1.0, 0.0).astype(BF16)
    zero_q = jnp.zeros((t, LANES), BF16)
    pairs = SB_HEADS // 2
    strict2 = jnp.concatenate([strict, strict], axis=0)
    qp = []
    for p in range(pairs):
        slab = q_ref[:, p * LANES:(p + 1) * LANES]
        qp.append(jnp.concatenate(
            [jnp.where(first_head, slab, zero_q), jnp.where(first_head, zero_q, slab)], axis=0))

    def tile(start, diag, carries):
        new = []
        for group in (range(0, pairs // 2), range(pairs // 2, pairs)):
            z = {p: _dot_t(qp[p], k_ref[pl.ds(start, t), p * LANES:(p + 1) * LANES]) for p in group}
            log_beta, neg_log_1mb = {}, {}
            for p in group:
                z_neg = jnp.minimum(z[p], 0.0)
                z_pos = z[p] - z_neg
                sp = jnp.log2(1.0 + jnp.exp2(z_neg - z_pos))
                log_beta[p] = z_neg - sp
                nl = z_pos + sp
                neg_log_1mb[p] = jnp.where(strict2, nl, 0.0) if diag else nl
            within = {p: _dot(neg_log_1mb[p].astype(BF16), later) for p in group}
            a = {}
            for p in group:
                ap = jnp.exp2(log_beta[p] - (within[p] + carries[p]))
                a[p] = (jnp.where(strict2, ap, 0.0) if diag else ap).astype(BF16)
            av = {p: _dot(a[p], v_ref[pl.ds(start, t), p * LANES:(p + 1) * LANES]) for p in group}
            for p in group:
                sl = slice(p * LANES, (p + 1) * LANES)
                contrib = jnp.where(first_head, av[p][:t], av[p][t:])
                if diag:
                    acc_sc[:, sl] = contrib
                else:
                    acc_sc[:, sl] += contrib
                new.append(carries[p] + (within[p][:, 0:1] + neg_log_1mb[p][:, 0:1]))
        return tuple(new)

    zc = jnp.zeros((2 * t, 1), F32)
    carries = tile(pl.multiple_of(i * t, t), True, (zc,) * pairs)

    def body(s, carries):
        return tile(pl.multiple_of((i - 1 - s) * t, t), False, carries)

    lax.fori_loop(0, i, body, carries)
    o_ref[...] = acc_sc[...].astype(BF16)


def _sb_call(q, k, v, batch, seq):
    t = SB_TILE
    nq = seq // t
    qmap = lambda b, i: (b * nq + i, 0)
    kvmap = lambda b, i: (b, 0)
    return pl.pallas_call(
        _sb_kernel,
        grid=(batch, nq),
        in_specs=[
            pl.BlockSpec((t, W_SB), qmap),
            pl.BlockSpec((seq, W_SB), kvmap),
            pl.BlockSpec((seq, W_SB), kvmap),
        ],
        out_specs=pl.BlockSpec((t, W_SB), qmap),
        out_shape=jax.ShapeDtypeStruct((batch * seq, W_SB), BF16),
        scratch_shapes=[pltpu.VMEM((t, W_SB), F32)],
        compiler_params=pltpu.CompilerParams(
            dimension_semantics=("arbitrary", "arbitrary"), vmem_limit_bytes=VMEM_LIMIT),
        name="stickbreak",
    )(q, k, v)


def _dsa_tile(width, i, dq_ref, iq_ref, iw_ref, dk_ref, dv_ref, ik_ref, o_ref, score_sc, bias_sc):
    tq = DSA_TQ
    lane = lax.broadcasted_iota(jnp.int32, (1, LANES), 1)
    half = lane // HEAD_DIM
    t_idx = i * tq + lax.broadcasted_iota(jnp.int32, (tq, 1), 0)

    iq = iq_ref[...]
    zero_q = jnp.zeros((tq, LANES), BF16)
    qstack = jnp.concatenate(
        [jnp.where(half == (h % 2), iq[:, (h // 2) * LANES:(h // 2 + 1) * LANES], zero_q)
         for h in range(IDX_HEADS)], axis=0)
    iw = iw_ref[...]
    wcol = [iw[:, h:h + 1] for h in range(IDX_HEADS)]
    lo = hi = None
    for c in range(width // DSA_KCHUNK):
        ksl = slice(c * DSA_KCHUNK, (c + 1) * DSA_KCHUNK)
        dots = _dot_t(qstack, ik_ref[ksl, :])
        score = jnp.zeros((tq, DSA_KCHUNK), F32)
        for h in range(IDX_HEADS):
            score = score + jnp.maximum(dots[h * tq:(h + 1) * tq], 0.0) * wcol[h]
        kpos = c * DSA_KCHUNK + lax.broadcasted_iota(jnp.int32, (1, DSA_KCHUNK), 1)
        causal = kpos <= t_idx
        score_sc[:, ksl] = jnp.where(causal, score, -jnp.inf)
        cmin = jnp.min(jnp.where(causal, score, jnp.inf), axis=1, keepdims=True)
        cmax = jnp.max(jnp.where(causal, score, -jnp.inf), axis=1, keepdims=True)
        lo = cmin if c == 0 else jnp.minimum(lo, cmin)
        hi = cmax if c == 0 else jnp.maximum(hi, cmax)

    k_row = jnp.minimum(t_idx + 1, TOPK_MAX).astype(F32)
    hi = hi + (jnp.abs(hi) + 1.0)

    def count(pred):
        return jnp.sum(jnp.where(pred, 1.0, 0.0), axis=1, keepdims=True)

    def bisect(it, bracket):
        lo, hi = bracket
        mid = lo + 0.5 * (hi - lo)
        ge = count(score_sc[:, :width] >= mid) >= k_row
        return jnp.where(ge, mid, lo), jnp.where(ge, hi, mid)

    lo, hi = lax.fori_loop(0, DSA_BISECT_STEPS, bisect, (lo, hi), unroll=DSA_BISECT_UNROLL)

    surplus = jnp.max(count(score_sc[:, :width] >= lo) - k_row)

    @pl.when(surplus <= 0.0)
    def _():
        bias_sc[:, :width] = jnp.where(score_sc[:, :width] >= lo, 0.0, NEG_BIG)

    @pl.when(surplus > 0.0)
    def _():
        lo_f, hi_f = lo, hi
        need = k_row - count(score_sc[:, :width] >= hi_f)
        r128 = lax.broadcasted_iota(jnp.int32, (LANES, LANES), 0)
        c128 = lax.broadcasted_iota(jnp.int32, (LANES, LANES), 1)
        earlier = jnp.where(r128 < c128, 1.0, 0.0).astype(BF16)
        ties_before = jnp.zeros((tq, 1), F32)
        for c in range(width // LANES):
            ksl = slice(c * LANES, (c + 1) * LANES)
            sc = score_sc[:, ksl]
            tied = jnp.where(sc >= lo_f, jnp.where(sc < hi_f, 1.0, 0.0), 0.0)
            rank = _dot(tied.astype(BF16), earlier) + ties_before
            tie_bias = jnp.where(rank < need, 0.0, NEG_BIG)
            bias_sc[:, ksl] = jnp.where(sc >= hi_f, 0.0, jnp.where(sc >= lo_f, tie_bias, NEG_BIG))
            ties_before = ties_before + jnp.sum(tied, axis=1, keepdims=True)

    dq = dq_ref[...]
    heads_per_group = DSA_HEADS // DSA_KV_HEADS
    outs = []
    for g in range(DSA_KV_HEADS):
        qg = jnp.concatenate(
            [jnp.where(half == g, dq[:, p * LANES:(p + 1) * LANES], zero_q)
             for p in range(heads_per_group)], axis=0)
        m = denom = acc = None
        for c in range(width // DSA_KCHUNK):
            ksl = slice(c * DSA_KCHUNK, (c + 1) * DSA_KCHUNK)
            logits = _dot_t(qg, dk_ref[ksl, :])
            logits = jnp.concatenate(
                [logits[p * tq:(p + 1) * tq] + bias_sc[:, ksl] for p in range(heads_per_group)], axis=0)
            cmax = jnp.max(logits, axis=1, keepdims=True)
            if c == 0:
                m = cmax
                pexp = jnp.exp(logits - m)
                denom = jnp.sum(pexp, axis=1, keepdims=True)
                acc = _dot(pexp.astype(BF16), dv_ref[ksl, :])
            else:
                m_new = jnp.maximum(m, cmax)
                alpha = jnp.exp(m - m_new)
                pexp = jnp.exp(logits - m_new)
                denom = alpha * denom + jnp.sum(pexp, axis=1, keepdims=True)
                acc = alpha * acc + _dot(pexp.astype(BF16), dv_ref[ksl, :])
                m = m_new
        outs.append(acc / denom)
    for p in range(heads_per_group):
        rows = slice(p * tq, (p + 1) * tq)
        o_ref[:, p * LANES:(p + 1) * LANES] = jnp.where(half == 0, outs[0][rows], outs[1][rows]).astype(BF16)


def _dsa_kernel(dq_ref, iq_ref, iw_ref, dk_ref, dv_ref, ik_ref, o_ref, score_sc, bias_sc, *, seq):
    i = pl.program_id(1)
    tiles_per_width = DSA_KCHUNK // DSA_TQ
    for v in range(seq // DSA_KCHUNK):
        @pl.when(i // tiles_per_width == v)
        def _(v=v):
            _dsa_tile((v + 1) * DSA_KCHUNK, i, dq_ref, iq_ref, iw_ref, dk_ref, dv_ref, ik_ref,
                      o_ref, score_sc, bias_sc)


def _dsa_call(dq, iq, iw, dk, dv, ik, batch, seq):
    tq = DSA_TQ
    nq = seq // tq
    qmap = lambda b, i: (b * nq + i, 0)
    kvmap = lambda b, i: (b, 0)
    return pl.pallas_call(
        functools.partial(_dsa_kernel, seq=seq),
        grid=(batch, nq),
        in_specs=[
            pl.BlockSpec((tq, 512), qmap),
            pl.BlockSpec((tq, 512), qmap),
            pl.BlockSpec((tq, LANES), qmap),
            pl.BlockSpec((seq, LANES), kvmap),
            pl.BlockSpec((seq, LANES), kvmap),
            pl.BlockSpec((seq, LANES), kvmap),
        ],
        out_specs=pl.BlockSpec((tq, 512), qmap),
        out_shape=jax.ShapeDtypeStruct((batch * seq, 512), BF16),
        scratch_shapes=[pltpu.VMEM((tq, seq), F32), pltpu.VMEM((tq, seq), F32)],
        compiler_params=pltpu.CompilerParams(
            dimension_semantics=("arbitrary", "arbitrary"), vmem_limit_bytes=VMEM_LIMIT),
        name="dsa",
    )(dq, iq, iw, dk, dv, ik)


def _memkv_kernel(m_ref, g_ref, w_ref, k_ref, v_ref):
    mb = _rms(m_ref[...], g_ref[...]).astype(BF16)
    kv = _dot(mb, w_ref[...])
    half = MEM_HEADS * MEM_HEAD_DIM
    k_ref[...] = kv[:, :half].astype(BF16)
    v_ref[...] = kv[:, half:].astype(BF16)


def _memkv_call(mem2, g, w_ckv, batch):
    half = MEM_HEADS * MEM_HEAD_DIM
    row = lambda b: (b, 0)
    const = lambda b: (0, 0)
    return pl.pallas_call(
        _memkv_kernel,
        grid=(batch,),
        in_specs=[
            pl.BlockSpec((N_MEM, D_MODEL), row),
            pl.BlockSpec((1, D_MODEL), const),
            pl.BlockSpec((D_MODEL, 2 * half), const),
        ],
        out_specs=[pl.BlockSpec((N_MEM, half), row)] * 2,
        out_shape=[jax.ShapeDtypeStruct((batch * N_MEM, half), BF16)] * 2,
        compiler_params=pltpu.CompilerParams(
            dimension_semantics=("arbitrary",), vmem_limit_bytes=VMEM_LIMIT),
        name="memkv",
    )(mem2, g, w_ckv)


def _mix_kernel(x_ref, osb_ref, odsa_ref, g1_ref, wg_ref, bg_ref, wbs_ref, wbd_ref, wo_ref,
                g2_ref, wcq_ref, km_ref, vm_ref, wco_ref, h_ref):
    x = x_ref[...]
    ub = _rms(x, g1_ref[...]).astype(BF16)
    gates = 1.0 / (1.0 + jnp.exp(-(_dot(ub, wg_ref[...]) + bg_ref[...])))
    merged = (gates[:, :D_MODEL] * _dot(osb_ref[...], wbs_ref[...])
              + gates[:, D_MODEL:] * _dot(odsa_ref[...], wbd_ref[...]))
    h1 = x + _dot(merged.astype(BF16), wo_ref[...])

    u2 = _rms(h1, g2_ref[...]).astype(BF16)
    qb = _dot(u2, wcq_ref[...]).astype(BF16)
    km = km_ref[...]
    vm = vm_ref[...]
    outs = []
    for h in range(MEM_HEADS):
        sl = slice(h * MEM_HEAD_DIM, (h + 1) * MEM_HEAD_DIM)
        logits = _dot_t(qb[:, sl], km[:, sl]) * (MEM_HEAD_DIM ** -0.5)
        m = jnp.max(logits, axis=1, keepdims=True)
        pexp = jnp.exp(logits - m)
        denom = jnp.sum(pexp, axis=1, keepdims=True)
        outs.append((_dot(pexp.astype(BF16), vm[:, sl]) / denom).astype(BF16))
    o = jnp.concatenate(outs, axis=1)
    h_ref[...] = h1 + _dot(o, wco_ref[...])


def _mix_call(x2, osb, odsa, g1, wg, bg, wbs, wbd, wo, g2, wcq, km, vm, wco, seq):
    n = x2.shape[0]
    tm = ROW_TILE
    tiles_per_seq = seq // tm
    half = MEM_HEADS * MEM_HEAD_DIM
    row = lambda i: (i, 0)
    const = lambda i: (0, 0)
    bat = lambda i: (i // tiles_per_seq, 0)

    def full(a):
        return pl.BlockSpec(a.shape, const)

    return pl.pallas_call(
        _mix_kernel,
        grid=(n // tm,),
        in_specs=[
            pl.BlockSpec((tm, D_MODEL), row),
            pl.BlockSpec((tm, 512), row),
            pl.BlockSpec((tm, 512), row),
            full(g1), full(wg), full(bg), full(wbs), full(wbd), full(wo),
            full(g2), full(wcq),
            pl.BlockSpec((N_MEM, half), bat),
            pl.BlockSpec((N_MEM, half), bat),
            full(wco),
        ],
        out_specs=pl.BlockSpec((tm, D_MODEL), row),
        out_shape=jax.ShapeDtypeStruct((n, D_MODEL), F32),
        compiler_params=pltpu.CompilerParams(
            dimension_semantics=("arbitrary",), vmem_limit_bytes=VMEM_LIMIT),
        name="mix",
    )(x2, osb, odsa, g1, wg, bg, wbs, wbd, wo, g2, wcq, km, vm, wco)


def _mlp_kernel(h_ref, g_ref, wu_ref, wd_ref, gf_ref, o_ref, *, final_norm):
    h = h_ref[...]
    ub = _rms(h, g_ref[...]).astype(BF16)
    acc = h
    for c in range(D_FF // D_MODEL):
        sl = slice(c * D_MODEL, (c + 1) * D_MODEL)
        hid = jnp.maximum(_dot(ub, wu_ref[:, sl]), 0.0)
        acc = acc + _dot((hid * hid).astype(BF16), wd_ref[sl, :])
    if final_norm:
        acc = _rms(acc, gf_ref[...])
    o_ref[...] = acc


def _mlp_call(h2, g, wu, wd, gf, final_norm):
    n = h2.shape[0]
    tm = ROW_TILE
    row = lambda i: (i, 0)
    const = lambda i: (0, 0)
    return pl.pallas_call(
        functools.partial(_mlp_kernel, final_norm=final_norm),
        grid=(n // tm,),
        in_specs=[
            pl.BlockSpec((tm, D_MODEL), row),
            pl.BlockSpec((1, D_MODEL), const),
            pl.BlockSpec((D_MODEL, D_FF), const),
            pl.BlockSpec((D_FF, D_MODEL), const),
            pl.BlockSpec((1, D_MODEL), const),
        ],
        out_specs=pl.BlockSpec((tm, D_MODEL), row),
        out_shape=jax.ShapeDtypeStruct((n, D_MODEL), F32),
        compiler_params=pltpu.CompilerParams(
            dimension_semantics=("arbitrary",), vmem_limit_bytes=VMEM_LIMIT),
        name="mlp",
    )(h2, g, wu, wd, gf)


def _pack_w_in(w_in):
    offs = np.cumsum([0, 512, 512, 512, 512, 128, 128, 512, 64, 8])
    sbq, sbk, sbv, dq, dk, dv, iq, ik, iw = [w_in[:, offs[j]:offs[j + 1]] for j in range(9)]
    dq = dq.reshape(D_MODEL, 2, 4, HEAD_DIM).transpose(0, 2, 1, 3).reshape(D_MODEL, 512)
    iw = jnp.pad(iw, ((0, 0), (0, LANES - IDX_HEADS)))
    return jnp.concatenate([sbq, sbk, sbv, dq, dk, dv, iq, ik, ik, iw], axis=1).astype(BF16)


def _rope_tables(seq):
    inv = ROPE_THETA ** (-jnp.arange(0, HEAD_DIM, 2, dtype=F32) / HEAD_DIM)
    ang = jnp.arange(seq).astype(F32)[:, None] * inv[None, :]
    cos = jnp.cos(ang)
    sin = jnp.sin(ang)
    cos_t = jnp.tile(cos, (1, LANES // HALF_ROT))
    sin_t = jnp.tile(jnp.concatenate([-sin, sin], axis=1), (1, LANES // HEAD_DIM))
    return cos_t, sin_t


def kernel(x, mem, norm_mix, w_in, w_branch_sb, w_branch_dsa, w_gate, b_gate, w_out, norm_cross,
           norm_mem, w_cq, w_ckv, w_co, norm_mlp, w_up, w_down, norm_final):
    batch, seq, d = x.shape
    depth = w_in.shape[0]
    cos_t, sin_t = _rope_tables(seq)
    h = x.reshape(batch * seq, d)
    mem2 = mem.reshape(batch * mem.shape[1], d)
    gf = norm_final.reshape(1, d)
    for l in range(depth):
        w_packed = _pack_w_in(w_in[l])
        wbd = (w_branch_dsa[l].reshape(2, 4, HEAD_DIM, d).transpose(1, 0, 2, 3)
               .reshape(DSA_HEADS * HEAD_DIM, d).astype(BF16))
        sbq, sbk, sbv, dq, dk, dv, iq, ik, iw = _proj_call(
            h, norm_mix[l].reshape(1, d), w_packed, cos_t, sin_t, seq)
        o_sb = _sb_call(sbq, sbk, sbv, batch, seq)
        o_dsa = _dsa_call(dq, iq, iw, dk, dv, ik, batch, seq)
        km, vm = _memkv_call(mem2, norm_mem[l].reshape(1, d), w_ckv[l].astype(BF16), batch)
        h2 = _mix_call(h, o_sb, o_dsa, norm_mix[l].reshape(1, d), w_gate[l].astype(BF16),
                       b_gate[l].reshape(1, -1), w_branch_sb[l].astype(BF16), wbd,
                       w_out[l].astype(BF16), norm_cross[l].reshape(1, d), w_cq[l].astype(BF16),
                       km, vm, w_co[l].astype(BF16), seq)
        h = _mlp_call(h2, norm_mlp[l].reshape(1, d), w_up[l].astype(BF16), w_down[l].astype(BF16),
                      gf, final_norm=(l == depth - 1))
    return h.reshape(batch, seq, d)
```

```python
import functools

import jax
import jax.numpy as jnp
import numpy as np
from jax import lax
from jax.experimental import pallas as pl
from jax.experimental.pallas import tpu as pltpu

F32 = jnp.float32
BF16 = jnp.bfloat16

D_MODEL = 1024
N_MEM = 256
SB_HEADS = 8
DSA_HEADS = 8
DSA_KV_HEADS = 2
IDX_HEADS = 8
HEAD_DIM = 64
TOPK_MAX = 256
MEM_HEADS = 4
MEM_HEAD_DIM = 128
D_FF = 4 * D_MODEL
ROPE_THETA = 10000.0
EPS = 1e-6

LANES = 128
HALF_ROT = HEAD_DIM // 2

W_SB = SB_HEADS * HEAD_DIM
OFF_SBQ = 0
OFF_SBK = OFF_SBQ + W_SB
OFF_SBV = OFF_SBK + W_SB
OFF_DQ = OFF_SBV + W_SB
OFF_DK = OFF_DQ + DSA_HEADS * HEAD_DIM
OFF_DV = OFF_DK + DSA_KV_HEADS * HEAD_DIM
OFF_IQ = OFF_DV + DSA_KV_HEADS * HEAD_DIM
OFF_IK = OFF_IQ + IDX_HEADS * HEAD_DIM
OFF_IW = OFF_IK + LANES
D_IN_PACKED = OFF_IW + LANES

ROW_TILE = 512
SB_TILE = 256
DSA_TQ = 256
DSA_KCHUNK = 512
VMEM_LIMIT = 56 * 1024 * 1024
NEG_BIG = -1e30
DSA_BISECT_STEPS = 32
DSA_BISECT_UNROLL = 4
LOG2_E = 1.4426950408889634


def _rms(x, g):
    return x * lax.rsqrt(jnp.mean(x * x, axis=-1, keepdims=True) + EPS) * g


def _dot(a, b):
    return jnp.dot(a, b, preferred_element_type=F32)


def _dot_t(a, b):
    return lax.dot_general(a, b, (((1,), (1,)), ((), ())), preferred_element_type=F32)


def _proj_kernel(x_ref, g_ref, w_ref, cos_ref, sin_ref,
                 sbq_ref, sbk_ref, sbv_ref, dq_ref, dk_ref, dv_ref, iq_ref, ik_ref, iw_ref):
    ub = _rms(x_ref[...], g_ref[...]).astype(BF16)
    cos = cos_ref[...]
    sin = sin_ref[...]
    lane = lax.broadcasted_iota(jnp.int32, (1, LANES), 1)
    first_half = (lane % HEAD_DIM) < HALF_ROT

    def mm(off, width=LANES):
        return _dot(ub, w_ref[:, off:off + width])

    def rope(s):
        rot = jnp.where(first_half, pltpu.roll(s, LANES - HALF_ROT, 1), pltpu.roll(s, HALF_ROT, 1))
        return s * cos + rot * sin

    q_scale = HEAD_DIM ** -0.5
    sbq_ref[...] = (mm(OFF_SBQ, W_SB) * (q_scale * LOG2_E)).astype(BF16)
    sbk_ref[...] = mm(OFF_SBK, W_SB).astype(BF16)
    sbv_ref[...] = mm(OFF_SBV, W_SB).astype(BF16)
    dq = mm(OFF_DQ, 4 * LANES)
    iq = mm(OFF_IQ, 4 * LANES)
    for p in range(4):
        sl = slice(p * LANES, (p + 1) * LANES)
        dq_ref[:, sl] = (rope(dq[:, sl]) * (q_scale * LOG2_E)).astype(BF16)
        iq_ref[:, sl] = (rope(iq[:, sl]) * q_scale).astype(BF16)
    dkv = mm(OFF_DK, 2 * LANES)
    dk_ref[...] = rope(dkv[:, :LANES]).astype(BF16)
    dv_ref[...] = dkv[:, LANES:].astype(BF16)
    ikw = mm(OFF_IK, 2 * LANES)
    ik_ref[...] = rope(ikw[:, :LANES]).astype(BF16)
    iw_ref[...] = ikw[:, LANES:] * (IDX_HEADS ** -0.5)


def _proj_call(x2, g, w_packed, cos_t, sin_t, seq):
    n = x2.shape[0]
    tm = ROW_TILE
    tiles_per_seq = seq // tm
    row = lambda i: (i, 0)
    const = lambda i: (0, 0)
    pos = lambda i: (i % tiles_per_seq, 0)
    out_w = [W_SB, W_SB, W_SB, 512, LANES, LANES, 512, LANES, LANES]
    out_dt = [BF16] * 8 + [F32]
    return pl.pallas_call(
        _proj_kernel,
        grid=(n // tm,),
        in_specs=[
            pl.BlockSpec((tm, D_MODEL), row),
            pl.BlockSpec((1, D_MODEL), const),
            pl.BlockSpec((D_MODEL, D_IN_PACKED), const),
            pl.BlockSpec((tm, LANES), pos),
            pl.BlockSpec((tm, LANES), pos),
        ],
        out_specs=[pl.BlockSpec((tm, w), row) for w in out_w],
        out_shape=[jax.ShapeDtypeStruct((n, w), dt) for w, dt in zip(out_w, out_dt)],
        compiler_params=pltpu.CompilerParams(
            dimension_semantics=("arbitrary",), vmem_limit_bytes=VMEM_LIMIT),
        name="proj",
    )(x2, g, w_packed, cos_t, sin_t)


def _sb_kernel(q_ref, k_ref, v_ref, o_ref, acc_sc):
    t = SB_TILE
    i = pl.program_id(1)
    lane = lax.broadcasted_iota(jnp.int32, (1, LANES), 1)
    first_head = lane < HEAD_DIM
    row = lax.broadcasted_iota(jnp.int32, (t, t), 0)
    col = lax.broadcasted_iota(jnp.int32, (t, t), 1)
    strict = col < row
    later = jnp.where(row > col, 1.0, 0.0).astype(BF16)
    zero_q = jnp.zeros((t, LANES), BF16)
    pairs = SB_HEADS // 2
    strict2 = jnp.concatenate([strict, strict], axis=0)
    qp = []
    for p in range(pairs):
        slab = q_ref[:, p * LANES:(p + 1) * LANES]
        qp.append(jnp.concatenate(
            [jnp.where(first_head, slab, zero_q), jnp.where(first_head, zero_q, slab)], axis=0))

    def tile(start, diag, carries):
        new = []
        for group in (range(0, pairs // 2), range(pairs // 2, pairs)):
            z = {p: _dot_t(qp[p], k_ref[pl.ds(start, t), p * LANES:(p + 1) * LANES]) for p in group}
            log_beta, neg_log_1mb = {}, {}
            for p in group:
                z_neg = jnp.minimum(z[p], 0.0)
                z_pos = z[p] - z_neg
                sp = jnp.log2(1.0 + jnp.exp2(z_neg - z_pos))
                log_beta[p] = z_neg - sp
                nl = z_pos + sp
                neg_log_1mb[p] = jnp.where(strict2, nl, 0.0) if diag else nl
            within = {p: _dot(neg_log_1mb[p].astype(BF16), later) for p in group}
            a = {}
            for p in group:
                ap = jnp.exp2(log_beta[p] - (within[p] + carries[p]))
                a[p] = (jnp.where(strict2, ap, 0.0) if diag else ap).astype(BF16)
            av = {p: _dot(a[p], v_ref[pl.ds(start, t), p * LANES:(p + 1) * LANES]) for p in group}
            for p in group:
                sl = slice(p * LANES, (p + 1) * LANES)
                contrib = jnp.where(first_head, av[p][:t], av[p][t:])
                if diag:
                    acc_sc[:, sl] = contrib
                else:
                    acc_sc[:, sl] += contrib
                new.append(carries[p] + (within[p][:, 0:1] + neg_log_1mb[p][:, 0:1]))
        return tuple(new)

    zc = jnp.zeros((2 * t, 1), F32)
    carries = tile(pl.multiple_of(i * t, t), True, (zc,) * pairs)

    def body(s, carries):
        return tile(pl.multiple_of((i - 1 - s) * t, t), False, carries)

    lax.fori_loop(0, i, body, carries)
    o_ref[...] = acc_sc[...].astype(BF16)


def _sb_call(q, k, v, batch, seq):
    t = SB_TILE
    nq = seq // t
    qmap = lambda b, i: (b * nq + i, 0)
    kvmap = lambda b, i: (b, 0)
    return pl.pallas_call(
        _sb_kernel,
        grid=(batch, nq),
        in_specs=[
            pl.BlockSpec((t, W_SB), qmap),
            pl.BlockSpec((seq, W_SB), kvmap),
            pl.BlockSpec((seq, W_SB), kvmap),
        ],
        out_specs=pl.BlockSpec((t, W_SB), qmap),
        out_shape=jax.ShapeDtypeStruct((batch * seq, W_SB), BF16),
        scratch_shapes=[pltpu.VMEM((t, W_SB), F32)],
        compiler_params=pltpu.CompilerParams(
            dimension_semantics=("arbitrary", "arbitrary"), vmem_limit_bytes=VMEM_LIMIT),
        name="stickbreak",
    )(q, k, v)


def _dsa_tile(width, i, dq_ref, iq_ref, iw_ref, dk_ref, dv_ref, ik_ref, o_ref, score_sc, bias_sc):
    tq = DSA_TQ
    lane = lax.broadcasted_iota(jnp.int32, (1, LANES), 1)
    half = lane // HEAD_DIM
    t_idx = i * tq + lax.broadcasted_iota(jnp.int32, (tq, 1), 0)

    iq = iq_ref[...]
    zero_q = jnp.zeros((tq, LANES), BF16)
    qstack = jnp.concatenate(
        [jnp.where(half == (h % 2), iq[:, (h // 2) * LANES:(h // 2 + 1) * LANES], zero_q)
         for h in range(IDX_HEADS)], axis=0)
    iw = iw_ref[...]
    wcol = [iw[:, h:h + 1] for h in range(IDX_HEADS)]
    lo = hi = None
    for c in range(width // DSA_KCHUNK):
        ksl = slice(c * DSA_KCHUNK, (c + 1) * DSA_KCHUNK)
        dots = _dot_t(qstack, ik_ref[ksl, :])
        score = jnp.zeros((tq, DSA_KCHUNK), F32)
        for h in range(IDX_HEADS):
            score = score + jnp.maximum(dots[h * tq:(h + 1) * tq], 0.0) * wcol[h]
        kpos = c * DSA_KCHUNK + lax.broadcasted_iota(jnp.int32, (1, DSA_KCHUNK), 1)
        causal = kpos <= t_idx
        score_sc[:, ksl] = jnp.where(causal, score, -jnp.inf)
        cmin = jnp.min(jnp.where(causal, score, jnp.inf), axis=1, keepdims=True)
        cmax = jnp.max(jnp.where(causal, score, -jnp.inf), axis=1, keepdims=True)
        lo = cmin if c == 0 else jnp.minimum(lo, cmin)
        hi = cmax if c == 0 else jnp.maximum(hi, cmax)

    k_row = jnp.minimum(t_idx + 1, TOPK_MAX).astype(F32)
    hi = hi + (jnp.abs(hi) + 1.0)

    def count(pred):
        return jnp.sum(jnp.where(pred, 1.0, 0.0), axis=1, keepdims=True)

    def bisect(it, bracket):
        lo, hi = bracket
        mid = lo + 0.5 * (hi - lo)
        ge = count(score_sc[:, :width] >= mid) >= k_row
        return jnp.where(ge, mid, lo), jnp.where(ge, hi, mid)

    lo, hi = lax.fori_loop(0, DSA_BISECT_STEPS, bisect, (lo, hi), unroll=DSA_BISECT_UNROLL)

    surplus = jnp.max(count(score_sc[:, :width] >= lo) - k_row)

    @pl.when(surplus <= 0.0)
    def _():
        bias_sc[:, :width] = jnp.where(score_sc[:, :width] >= lo, 0.0, NEG_BIG)

    @pl.when(surplus > 0.0)
    def _():
        lo_f, hi_f = lo, hi
        need = k_row - count(score_sc[:, :width] >= hi_f)
        r128 = lax.broadcasted_iota(jnp.int32, (LANES, LANES), 0)
        c128 = lax.broadcasted_iota(jnp.int32, (LANES, LANES), 1)
        earlier = jnp.where(r128 < c128, 1.0, 0.0).astype(BF16)
        ties_before = jnp.zeros((tq, 1), F32)
        for c in range(width // LANES):
            ksl = slice(c * LANES, (c + 1) * LANES)
            sc = score_sc[:, ksl]
            tied = jnp.where(sc >= lo_f, jnp.where(sc < hi_f, 1.0, 0.0), 0.0)
            rank = _dot(tied.astype(BF16), earlier) + ties_before
            tie_bias = jnp.where(rank < need, 0.0, NEG_BIG)
            bias_sc[:, ksl] = jnp.where(sc >= hi_f, 0.0, jnp.where(sc >= lo_f, tie_bias, NEG_BIG))
            ties_before = ties_before + jnp.sum(tied, axis=1, keepdims=True)

    dq = dq_ref[...]
    heads_per_group = DSA_HEADS // DSA_KV_HEADS
    outs = []
    for g in range(DSA_KV_HEADS):
        qg = jnp.concatenate(
            [jnp.where(half == g, dq[:, p * LANES:(p + 1) * LANES], zero_q)
             for p in range(heads_per_group)], axis=0)
        m = denom = acc = None
        for c in range(width // DSA_KCHUNK):
            ksl = slice(c * DSA_KCHUNK, (c + 1) * DSA_KCHUNK)
            logits = _dot_t(qg, dk_ref[ksl, :])
            logits = jnp.concatenate(
                [logits[p * tq:(p + 1) * tq] + bias_sc[:, ksl] for p in range(heads_per_group)], axis=0)
            cmax = jnp.max(logits, axis=1, keepdims=True)
            if c == 0:
                m = cmax
                pexp = jnp.exp2(logits - m)
                denom = jnp.sum(pexp, axis=1, keepdims=True)
                acc = _dot(pexp.astype(BF16), dv_ref[ksl, :])
            else:
                m_new = jnp.maximum(m, cmax)
                alpha = jnp.exp2(m - m_new)
                pexp = jnp.exp2(logits - m_new)
                denom = alpha * denom + jnp.sum(pexp, axis=1, keepdims=True)
                acc = alpha * acc + _dot(pexp.astype(BF16), dv_ref[ksl, :])
                m = m_new
        outs.append(acc / denom)
    for p in range(heads_per_group):
        rows = slice(p * tq, (p + 1) * tq)
        o_ref[:, p * LANES:(p + 1) * LANES] = jnp.where(half == 0, outs[0][rows], outs[1][rows]).astype(BF16)


def _dsa_kernel(dq_ref, iq_ref, iw_ref, dk_ref, dv_ref, ik_ref, o_ref, score_sc, bias_sc, *, seq):
    i = pl.program_id(1)
    tiles_per_width = DSA_KCHUNK // DSA_TQ
    for v in range(seq // DSA_KCHUNK):
        @pl.when(i // tiles_per_width == v)
        def _(v=v):
            _dsa_tile((v + 1) * DSA_KCHUNK, i, dq_ref, iq_ref, iw_ref, dk_ref, dv_ref, ik_ref,
                      o_ref, score_sc, bias_sc)


def _dsa_call(dq, iq, iw, dk, dv, ik, batch, seq):
    tq = DSA_TQ
    nq = seq // tq
    qmap = lambda b, i: (b * nq + i, 0)
    kvmap = lambda b, i: (b, 0)
    return pl.pallas_call(
        functools.partial(_dsa_kernel, seq=seq),
        grid=(batch, nq),
        in_specs=[
            pl.BlockSpec((tq, 512), qmap),
            pl.BlockSpec((tq, 512), qmap),
            pl.BlockSpec((tq, LANES), qmap),
            pl.BlockSpec((seq, LANES), kvmap),
            pl.BlockSpec((seq, LANES), kvmap),
            pl.BlockSpec((seq, LANES), kvmap),
        ],
        out_specs=pl.BlockSpec((tq, 512), qmap),
        out_shape=jax.ShapeDtypeStruct((batch * seq, 512), BF16),
        scratch_shapes=[pltpu.VMEM((tq, seq), F32), pltpu.VMEM((tq, seq), F32)],
        compiler_params=pltpu.CompilerParams(
            dimension_semantics=("arbitrary", "arbitrary"), vmem_limit_bytes=VMEM_LIMIT),
        name="dsa",
    )(dq, iq, iw, dk, dv, ik)


def _memkv_kernel(m_ref, g_ref, w_ref, k_ref, v_ref):
    mb = _rms(m_ref[...], g_ref[...]).astype(BF16)
    kv = _dot(mb, w_ref[...])
    half = MEM_HEADS * MEM_HEAD_DIM
    k_ref[...] = kv[:, :half].astype(BF16)
    v_ref[...] = kv[:, half:].astype(BF16)


def _memkv_call(mem2, g, w_ckv, batch):
    half = MEM_HEADS * MEM_HEAD_DIM
    row = lambda b: (b, 0)
    const = lambda b: (0, 0)
    return pl.pallas_call(
        _memkv_kernel,
        grid=(batch,),
        in_specs=[
            pl.BlockSpec((N_MEM, D_MODEL), row),
            pl.BlockSpec((1, D_MODEL), const),
            pl.BlockSpec((D_MODEL, 2 * half), const),
        ],
        out_specs=[pl.BlockSpec((N_MEM, half), row)] * 2,
        out_shape=[jax.ShapeDtypeStruct((batch * N_MEM, half), BF16)] * 2,
        compiler_params=pltpu.CompilerParams(
            dimension_semantics=("arbitrary",), vmem_limit_bytes=VMEM_LIMIT),
        name="memkv",
    )(mem2, g, w_ckv)


def _mix_kernel(x_ref, osb_ref, odsa_ref, g1_ref, wg_ref, bg_ref, wbs_ref, wbd_ref, wo_ref,
                g2_ref, wcq_ref, km_ref, vm_ref, wco_ref, h_ref):
    x = x_ref[...]
    ub = _rms(x, g1_ref[...]).astype(BF16)
    gates = 1.0 / (1.0 + jnp.exp(-(_dot(ub, wg_ref[...]) + bg_ref[...])))
    merged = (gates[:, :D_MODEL] * _dot(osb_ref[...], wbs_ref[...])
              + gates[:, D_MODEL:] * _dot(odsa_ref[...], wbd_ref[...]))
    h1 = x + _dot(merged.astype(BF16), wo_ref[...])

    u2 = _rms(h1, g2_ref[...]).astype(BF16)
    qb = _dot(u2, wcq_ref[...]).astype(BF16)
    km = km_ref[...]
    vm = vm_ref[...]
    outs = []
    for h in range(MEM_HEADS):
        sl = slice(h * MEM_HEAD_DIM, (h + 1) * MEM_HEAD_DIM)
        logits = _dot_t(qb[:, sl], km[:, sl]) * (MEM_HEAD_DIM ** -0.5)
        m = jnp.max(logits, axis=1, keepdims=True)
        pexp = jnp.exp(logits - m)
        denom = jnp.sum(pexp, axis=1, keepdims=True)
        outs.append((_dot(pexp.astype(BF16), vm[:, sl]) / denom).astype(BF16))
    o = jnp.concatenate(outs, axis=1)
    h_ref[...] = h1 + _dot(o, wco_ref[...])


def _mix_call(x2, osb, odsa, g1, wg, bg, wbs, wbd, wo, g2, wcq, km, vm, wco, seq):
    n = x2.shape[0]
    tm = ROW_TILE
    tiles_per_seq = seq // tm
    half = MEM_HEADS * MEM_HEAD_DIM
    row = lambda i: (i, 0)
    const = lambda i: (0, 0)
    bat = lambda i: (i // tiles_per_seq, 0)

    def full(a):
        return pl.BlockSpec(a.shape, const)

    return pl.pallas_call(
        _mix_kernel,
        grid=(n // tm,),
        in_specs=[
            pl.BlockSpec((tm, D_MODEL), row),
            pl.BlockSpec((tm, 512), row),
            pl.BlockSpec((tm, 512), row),
            full(g1), full(wg), full(bg), full(wbs), full(wbd), full(wo),
            full(g2), full(wcq),
            pl.BlockSpec((N_MEM, half), bat),
            pl.BlockSpec((N_MEM, half), bat),
            full(wco),
        ],
        out_specs=pl.BlockSpec((tm, D_MODEL), row),
        out_shape=jax.ShapeDtypeStruct((n, D_MODEL), F32),
        compiler_params=pltpu.CompilerParams(
            dimension_semantics=("arbitrary",), vmem_limit_bytes=VMEM_LIMIT),
        name="mix",
    )(x2, osb, odsa, g1, wg, bg, wbs, wbd, wo, g2, wcq, km, vm, wco)


def _mlp_kernel(h_ref, g_ref, wu_ref, wd_ref, gf_ref, o_ref, *, final_norm):
    h = h_ref[...]
    ub = _rms(h, g_ref[...]).astype(BF16)
    acc = h
    for c in range(D_FF // D_MODEL):
        sl = slice(c * D_MODEL, (c + 1) * D_MODEL)
        hid = jnp.maximum(_dot(ub, wu_ref[:, sl]), 0.0)
        acc = acc + _dot((hid * hid).astype(BF16), wd_ref[sl, :])
    if final_norm:
        acc = _rms(acc, gf_ref[...])
    o_ref[...] = acc


def _mlp_call(h2, g, wu, wd, gf, final_norm):
    n = h2.shape[0]
    tm = ROW_TILE
    row = lambda i: (i, 0)
    const = lambda i: (0, 0)
    return pl.pallas_call(
        functools.partial(_mlp_kernel, final_norm=final_norm),
        grid=(n // tm,),
        in_specs=[
            pl.BlockSpec((tm, D_MODEL), row),
            pl.BlockSpec((1, D_MODEL), const),
            pl.BlockSpec((D_MODEL, D_FF), const),
            pl.BlockSpec((D_FF, D_MODEL), const),
            pl.BlockSpec((1, D_MODEL), const),
        ],
        out_specs=pl.BlockSpec((tm, D_MODEL), row),
        out_shape=jax.ShapeDtypeStruct((n, D_MODEL), F32),
        compiler_params=pltpu.CompilerParams(
            dimension_semantics=("arbitrary",), vmem_limit_bytes=VMEM_LIMIT),
        name="mlp",
    )(h2, g, wu, wd, gf)


def _pack_w_in(w_in):
    offs = np.cumsum([0, 512, 512, 512, 512, 128, 128, 512, 64, 8])
    sbq, sbk, sbv, dq, dk, dv, iq, ik, iw = [w_in[:, offs[j]:offs[j + 1]] for j in range(9)]
    dq = dq.reshape(D_MODEL, 2, 4, HEAD_DIM).transpose(0, 2, 1, 3).reshape(D_MODEL, 512)
    iw = jnp.pad(iw, ((0, 0), (0, LANES - IDX_HEADS)))
    return jnp.concatenate([sbq, sbk, sbv, dq, dk, dv, iq, ik, ik, iw], axis=1).astype(BF16)


def _rope_tables(seq):
    inv = ROPE_THETA ** (-jnp.arange(0, HEAD_DIM, 2, dtype=F32) / HEAD_DIM)
    ang = jnp.arange(seq).astype(F32)[:, None] * inv[None, :]
    cos = jnp.cos(ang)
    sin = jnp.sin(ang)
    cos_t = jnp.tile(cos, (1, LANES // HALF_ROT))
    sin_t = jnp.tile(jnp.concatenate([-sin, sin], axis=1), (1, LANES // HEAD_DIM))
    return cos_t, sin_t


def kernel(x, mem, norm_mix, w_in, w_branch_sb, w_branch_dsa, w_gate, b_gate, w_out, norm_cross,
           norm_mem, w_cq, w_ckv, w_co, norm_mlp, w_up, w_down, norm_final):
    batch, seq, d = x.shape
    depth = w_in.shape[0]
    cos_t, sin_t = _rope_tables(seq)
    h = x.reshape(batch * seq, d)
    mem2 = mem.reshape(batch * mem.shape[1], d)
    gf = norm_final.reshape(1, d)
    for l in range(depth):
        w_packed = _pack_w_in(w_in[l])
        wbd = (w_branch_dsa[l].reshape(2, 4, HEAD_DIM, d).transpose(1, 0, 2, 3)
               .reshape(DSA_HEADS * HEAD_DIM, d).astype(BF16))
        sbq, sbk, sbv, dq, dk, dv, iq, ik, iw = _proj_call(
            h, norm_mix[l].reshape(1, d), w_packed, cos_t, sin_t, seq)
        o_sb = _sb_call(sbq, sbk, sbv, batch, seq)
        o_dsa = _dsa_call(dq, iq, iw, dk, dv, ik, batch, seq)
        km, vm = _memkv_call(mem2, norm_mem[l].reshape(1, d), w_ckv[l].astype(BF16), batch)
        h2 = _mix_call(h, o_sb, o_dsa, norm_mix[l].reshape(1, d), w_gate[l].astype(BF16),
                       b_gate[l].reshape(1, -1), w_branch_sb[l].astype(BF16), wbd,
                       w_out[l].astype(BF16), norm_cross[l].reshape(1, d), w_cq[l].astype(BF16),
                       km, vm, w_co[l].astype(BF16), seq)
        h = _mlp_call(h2, norm_mlp[l].reshape(1, d), w_up[l].astype(BF16), w_down[l].astype(BF16),
                      gf, final_norm=(l == depth - 1))
    return h.reshape(batch, seq, d)
```

```python
import functools

import jax
import jax.numpy as jnp
import numpy as np
from jax import lax
from jax.experimental import pallas as pl
from jax.experimental.pallas import tpu as pltpu

F32 = jnp.float32
BF16 = jnp.bfloat16

D_MODEL = 1024
N_MEM = 256
SB_HEADS = 8
DSA_HEADS = 8
DSA_KV_HEADS = 2
IDX_HEADS = 8
HEAD_DIM = 64
TOPK_MAX = 256
MEM_HEADS = 4
MEM_HEAD_DIM = 128
D_FF = 4 * D_MODEL
ROPE_THETA = 10000.0
EPS = 1e-6

LANES = 128
HALF_ROT = HEAD_DIM // 2

W_SB = SB_HEADS * HEAD_DIM
OFF_SBQ = 0
OFF_SBK = OFF_SBQ + W_SB
OFF_SBV = OFF_SBK + W_SB
OFF_DQ = OFF_SBV + W_SB
OFF_DK = OFF_DQ + DSA_HEADS * HEAD_DIM
OFF_DV = OFF_DK + DSA_KV_HEADS * HEAD_DIM
OFF_IQ = OFF_DV + DSA_KV_HEADS * HEAD_DIM
OFF_IK = OFF_IQ + IDX_HEADS * HEAD_DIM
OFF_IW = OFF_IK + LANES
D_IN_PACKED = OFF_IW + LANES

ROW_TILE = 512
SB_TILE = 256
DSA_TQ = 256
DSA_KCHUNK = 512
VMEM_LIMIT = 56 * 1024 * 1024
NEG_BIG = -1e30
DSA_BISECT_STEPS = 32
DSA_BISECT_UNROLL = 8
LOG2_E = 1.4426950408889634


def _rms(x, g):
    return x * lax.rsqrt(jnp.mean(x * x, axis=-1, keepdims=True) + EPS) * g


def _dot(a, b):
    return jnp.dot(a, b, preferred_element_type=F32)


def _dot_t(a, b):
    return lax.dot_general(a, b, (((1,), (1,)), ((), ())), preferred_element_type=F32)


def _proj_kernel(x_ref, g_ref, w_ref, cos_ref, sin_ref,
                 sbq_ref, sbk_ref, sbv_ref, dq_ref, dk_ref, dv_ref, iq_ref, ik_ref, iw_ref):
    ub = _rms(x_ref[...], g_ref[...]).astype(BF16)
    cos = cos_ref[...]
    sin = sin_ref[...]
    lane = lax.broadcasted_iota(jnp.int32, (1, LANES), 1)
    first_half = (lane % HEAD_DIM) < HALF_ROT

    def mm(off, width=LANES):
        return _dot(ub, w_ref[:, off:off + width])

    def rope(s):
        rot = jnp.where(first_half, pltpu.roll(s, LANES - HALF_ROT, 1), pltpu.roll(s, HALF_ROT, 1))
        return s * cos + rot * sin

    q_scale = HEAD_DIM ** -0.5
    sbq_ref[...] = (mm(OFF_SBQ, W_SB) * (q_scale * LOG2_E)).astype(BF16)
    sbk_ref[...] = mm(OFF_SBK, W_SB).astype(BF16)
    sbv_ref[...] = mm(OFF_SBV, W_SB).astype(BF16)
    dq = mm(OFF_DQ, 4 * LANES)
    iq = mm(OFF_IQ, 4 * LANES)
    for p in range(4):
        sl = slice(p * LANES, (p + 1) * LANES)
        dq_ref[:, sl] = (rope(dq[:, sl]) * (q_scale * LOG2_E)).astype(BF16)
        iq_ref[:, sl] = (rope(iq[:, sl]) * q_scale).astype(BF16)
    dkv = mm(OFF_DK, 2 * LANES)
    dk_ref[...] = rope(dkv[:, :LANES]).astype(BF16)
    dv_ref[...] = dkv[:, LANES:].astype(BF16)
    ikw = mm(OFF_IK, 2 * LANES)
    ik_ref[...] = rope(ikw[:, :LANES]).astype(BF16)
    iw_ref[...] = ikw[:, LANES:] * (IDX_HEADS ** -0.5)


def _proj_call(x2, g, w_packed, cos_t, sin_t, seq):
    n = x2.shape[0]
    tm = ROW_TILE
    tiles_per_seq = seq // tm
    row = lambda i: (i, 0)
    const = lambda i: (0, 0)
    pos = lambda i: (i % tiles_per_seq, 0)
    out_w = [W_SB, W_SB, W_SB, 512, LANES, LANES, 512, LANES, LANES]
    out_dt = [BF16] * 8 + [F32]
    return pl.pallas_call(
        _proj_kernel,
        grid=(n // tm,),
        in_specs=[
            pl.BlockSpec((tm, D_MODEL), row),
            pl.BlockSpec((1, D_MODEL), const),
            pl.BlockSpec((D_MODEL, D_IN_PACKED), const),
            pl.BlockSpec((tm, LANES), pos),
            pl.BlockSpec((tm, LANES), pos),
        ],
        out_specs=[pl.BlockSpec((tm, w), row) for w in out_w],
        out_shape=[jax.ShapeDtypeStruct((n, w), dt) for w, dt in zip(out_w, out_dt)],
        compiler_params=pltpu.CompilerParams(
            dimension_semantics=("arbitrary",), vmem_limit_bytes=VMEM_LIMIT),
        name="proj",
    )(x2, g, w_packed, cos_t, sin_t)


def _sb_kernel(q_ref, k_ref, v_ref, o_ref, acc_sc):
    t = SB_TILE
    i = pl.program_id(1)
    lane = lax.broadcasted_iota(jnp.int32, (1, LANES), 1)
    first_head = lane < HEAD_DIM
    row = lax.broadcasted_iota(jnp.int32, (t, t), 0)
    col = lax.broadcasted_iota(jnp.int32, (t, t), 1)
    strict = col < row
    later = jnp.where(row > col, 1.0, 0.0).astype(BF16)
    zero_q = jnp.zeros((t, LANES), BF16)
    pairs = SB_HEADS // 2
    strict2 = jnp.concatenate([strict, strict], axis=0)
    qp = []
    for p in range(pairs):
        slab = q_ref[:, p * LANES:(p + 1) * LANES]
        qp.append(jnp.concatenate(
            [jnp.where(first_head, slab, zero_q), jnp.where(first_head, zero_q, slab)], axis=0))

    def tile(start, diag, carries):
        new = []
        for group in (range(0, pairs // 2), range(pairs // 2, pairs)):
            z = {p: _dot_t(qp[p], k_ref[pl.ds(start, t), p * LANES:(p + 1) * LANES]) for p in group}
            log_beta, neg_log_1mb = {}, {}
            for p in group:
                z_neg = jnp.minimum(z[p], 0.0)
                z_pos = z[p] - z_neg
                sp = jnp.log2(1.0 + jnp.exp2(z_neg - z_pos))
                log_beta[p] = z_neg - sp
                nl = z_pos + sp
                neg_log_1mb[p] = jnp.where(strict2, nl, 0.0) if diag else nl
            within = {p: _dot(neg_log_1mb[p].astype(BF16), later) for p in group}
            a = {}
            for p in group:
                ap = jnp.exp2(log_beta[p] - (within[p] + carries[p]))
                a[p] = (jnp.where(strict2, ap, 0.0) if diag else ap).astype(BF16)
            av = {p: _dot(a[p], v_ref[pl.ds(start, t), p * LANES:(p + 1) * LANES]) for p in group}
            for p in group:
                sl = slice(p * LANES, (p + 1) * LANES)
                contrib = jnp.where(first_head, av[p][:t], av[p][t:])
                if diag:
                    acc_sc[:, sl] = contrib
                else:
                    acc_sc[:, sl] += contrib
                new.append(carries[p] + (within[p][:, 0:1] + neg_log_1mb[p][:, 0:1]))
        return tuple(new)

    zc = jnp.zeros((2 * t, 1), F32)
    carries = tile(pl.multiple_of(i * t, t), True, (zc,) * pairs)

    def body(s, carries):
        return tile(pl.multiple_of((i - 1 - s) * t, t), False, carries)

    lax.fori_loop(0, i, body, carries)
    o_ref[...] = acc_sc[...].astype(BF16)


def _sb_call(q, k, v, batch, seq):
    t = SB_TILE
    nq = seq // t
    qmap = lambda b, i: (b * nq + i, 0)
    kvmap = lambda b, i: (b, 0)
    return pl.pallas_call(
        _sb_kernel,
        grid=(batch, nq),
        in_specs=[
            pl.BlockSpec((t, W_SB), qmap),
            pl.BlockSpec((seq, W_SB), kvmap),
            pl.BlockSpec((seq, W_SB), kvmap),
        ],
        out_specs=pl.BlockSpec((t, W_SB), qmap),
        out_shape=jax.ShapeDtypeStruct((batch * seq, W_SB), BF16),
        scratch_shapes=[pltpu.VMEM((t, W_SB), F32)],
        compiler_params=pltpu.CompilerParams(
            dimension_semantics=("arbitrary", "arbitrary"), vmem_limit_bytes=VMEM_LIMIT),
        name="stickbreak",
    )(q, k, v)


def _dsa_tile(width, i, dq_ref, iq_ref, iw_ref, dk_ref, dv_ref, ik_ref, o_ref, score_sc, bias_sc):
    tq = DSA_TQ
    lane = lax.broadcasted_iota(jnp.int32, (1, LANES), 1)
    half = lane // HEAD_DIM
    t_idx = i * tq + lax.broadcasted_iota(jnp.int32, (tq, 1), 0)

    iq = iq_ref[...]
    zero_q = jnp.zeros((tq, LANES), BF16)
    qstack = jnp.concatenate(
        [jnp.where(half == (h % 2), iq[:, (h // 2) * LANES:(h // 2 + 1) * LANES], zero_q)
         for h in range(IDX_HEADS)], axis=0)
    iw = iw_ref[...]
    wcol = [iw[:, h:h + 1] for h in range(IDX_HEADS)]
    lo = hi = None
    for c in range(width // DSA_KCHUNK):
        ksl = slice(c * DSA_KCHUNK, (c + 1) * DSA_KCHUNK)
        dots = _dot_t(qstack, ik_ref[ksl, :])
        score = jnp.zeros((tq, DSA_KCHUNK), F32)
        for h in range(IDX_HEADS):
            score = score + jnp.maximum(dots[h * tq:(h + 1) * tq], 0.0) * wcol[h]
        kpos = c * DSA_KCHUNK + lax.broadcasted_iota(jnp.int32, (1, DSA_KCHUNK), 1)
        causal = kpos <= t_idx
        score_sc[:, ksl] = jnp.where(causal, score, -jnp.inf)
        cmin = jnp.min(jnp.where(causal, score, jnp.inf), axis=1, keepdims=True)
        cmax = jnp.max(jnp.where(causal, score, -jnp.inf), axis=1, keepdims=True)
        lo = cmin if c == 0 else jnp.minimum(lo, cmin)
        hi = cmax if c == 0 else jnp.maximum(hi, cmax)

    k_row = jnp.minimum(t_idx + 1, TOPK_MAX).astype(F32)
    hi = hi + (jnp.abs(hi) + 1.0)

    def count(pred):
        return jnp.sum(jnp.where(pred, 1.0, 0.0), axis=1, keepdims=True)

    def bisect(it, bracket):
        lo, hi = bracket
        mid = lo + 0.5 * (hi - lo)
        ge = count(score_sc[:, :width] >= mid) >= k_row
        return jnp.where(ge, mid, lo), jnp.where(ge, hi, mid)

    lo, hi = lax.fori_loop(0, DSA_BISECT_STEPS, bisect, (lo, hi), unroll=DSA_BISECT_UNROLL)

    surplus = jnp.max(count(score_sc[:, :width] >= lo) - k_row)

    @pl.when(surplus <= 0.0)
    def _():
        bias_sc[:, :width] = jnp.where(score_sc[:, :width] >= lo, 0.0, NEG_BIG)

    @pl.when(surplus > 0.0)
    def _():
        lo_f, hi_f = lo, hi
        need = k_row - count(score_sc[:, :width] >= hi_f)
        r128 = lax.broadcasted_iota(jnp.int32, (LANES, LANES), 0)
        c128 = lax.broadcasted_iota(jnp.int32, (LANES, LANES), 1)
        earlier = jnp.where(r128 < c128, 1.0, 0.0).astype(BF16)
        ties_before = jnp.zeros((tq, 1), F32)
        for c in range(width // LANES):
            ksl = slice(c * LANES, (c + 1) * LANES)
            sc = score_sc[:, ksl]
            tied = jnp.where(sc >= lo_f, jnp.where(sc < hi_f, 1.0, 0.0), 0.0)
            rank = _dot(tied.astype(BF16), earlier) + ties_before
            tie_bias = jnp.where(rank < need, 0.0, NEG_BIG)
            bias_sc[:, ksl] = jnp.where(sc >= hi_f, 0.0, jnp.where(sc >= lo_f, tie_bias, NEG_BIG))
            ties_before = ties_before + jnp.sum(tied, axis=1, keepdims=True)

    dq = dq_ref[...]
    heads_per_group = DSA_HEADS // DSA_KV_HEADS
    outs = []
    for g in range(DSA_KV_HEADS):
        qg = jnp.concatenate(
            [jnp.where(half == g, dq[:, p * LANES:(p + 1) * LANES], zero_q)
             for p in range(heads_per_group)], axis=0)
        m = denom = acc = None
        for c in range(width // DSA_KCHUNK):
            ksl = slice(c * DSA_KCHUNK, (c + 1) * DSA_KCHUNK)
            logits = _dot_t(qg, dk_ref[ksl, :])
            logits = jnp.concatenate(
                [logits[p * tq:(p + 1) * tq] + bias_sc[:, ksl] for p in range(heads_per_group)], axis=0)
            cmax = jnp.max(logits, axis=1, keepdims=True)
            if c == 0:
                m = cmax
                pexp = jnp.exp2(logits - m)
                denom = jnp.sum(pexp, axis=1, keepdims=True)
                acc = _dot(pexp.astype(BF16), dv_ref[ksl, :])
            else:
                m_new = jnp.maximum(m, cmax)
                alpha = jnp.exp2(m - m_new)
                pexp = jnp.exp2(logits - m_new)
                denom = alpha * denom + jnp.sum(pexp, axis=1, keepdims=True)
                acc = alpha * acc + _dot(pexp.astype(BF16), dv_ref[ksl, :])
                m = m_new
        outs.append(acc / denom)
    for p in range(heads_per_group):
        rows = slice(p * tq, (p + 1) * tq)
        o_ref[:, p * LANES:(p + 1) * LANES] = jnp.where(half == 0, outs[0][rows], outs[1][rows]).astype(BF16)


def _dsa_kernel(dq_ref, iq_ref, iw_ref, dk_ref, dv_ref, ik_ref, o_ref, score_sc, bias_sc, *, seq):
    i = pl.program_id(1)
    tiles_per_width = DSA_KCHUNK // DSA_TQ
    for v in range(seq // DSA_KCHUNK):
        @pl.when(i // tiles_per_width == v)
        def _(v=v):
            _dsa_tile((v + 1) * DSA_KCHUNK, i, dq_ref, iq_ref, iw_ref, dk_ref, dv_ref, ik_ref,
                      o_ref, score_sc, bias_sc)


def _dsa_call(dq, iq, iw, dk, dv, ik, batch, seq):
    tq = DSA_TQ
    nq = seq // tq
    qmap = lambda b, i: (b * nq + i, 0)
    kvmap = lambda b, i: (b, 0)
    return pl.pallas_call(
        functools.partial(_dsa_kernel, seq=seq),
        grid=(batch, nq),
        in_specs=[
            pl.BlockSpec((tq, 512), qmap),
            pl.BlockSpec((tq, 512), qmap),
            pl.BlockSpec((tq, LANES), qmap),
            pl.BlockSpec((seq, LANES), kvmap),
            pl.BlockSpec((seq, LANES), kvmap),
            pl.BlockSpec((seq, LANES), kvmap),
        ],
        out_specs=pl.BlockSpec((tq, 512), qmap),
        out_shape=jax.ShapeDtypeStruct((batch * seq, 512), BF16),
        scratch_shapes=[pltpu.VMEM((tq, seq), F32), pltpu.VMEM((tq, seq), F32)],
        compiler_params=pltpu.CompilerParams(
            dimension_semantics=("arbitrary", "arbitrary"), vmem_limit_bytes=VMEM_LIMIT),
        name="dsa",
    )(dq, iq, iw, dk, dv, ik)


def _memkv_kernel(m_ref, g_ref, w_ref, k_ref, v_ref):
    mb = _rms(m_ref[...], g_ref[...]).astype(BF16)
    kv = _dot(mb, w_ref[...])
    half = MEM_HEADS * MEM_HEAD_DIM
    k_ref[...] = kv[:, :half].astype(BF16)
    v_ref[...] = kv[:, half:].astype(BF16)


def _memkv_call(mem2, g, w_ckv, batch):
    half = MEM_HEADS * MEM_HEAD_DIM
    row = lambda b: (b, 0)
    const = lambda b: (0, 0)
    return pl.pallas_call(
        _memkv_kernel,
        grid=(batch,),
        in_specs=[
            pl.BlockSpec((N_MEM, D_MODEL), row),
            pl.BlockSpec((1, D_MODEL), const),
            pl.BlockSpec((D_MODEL, 2 * half), const),
        ],
        out_specs=[pl.BlockSpec((N_MEM, half), row)] * 2,
        out_shape=[jax.ShapeDtypeStruct((batch * N_MEM, half), BF16)] * 2,
        compiler_params=pltpu.CompilerParams(
            dimension_semantics=("arbitrary",), vmem_limit_bytes=VMEM_LIMIT),
        name="memkv",
    )(mem2, g, w_ckv)


def _mix_kernel(x_ref, osb_ref, odsa_ref, g1_ref, wg_ref, bg_ref, wbs_ref, wbd_ref, wo_ref,
                g2_ref, wcq_ref, km_ref, vm_ref, wco_ref, h_ref):
    x = x_ref[...]
    ub = _rms(x, g1_ref[...]).astype(BF16)
    gates = 1.0 / (1.0 + jnp.exp(-(_dot(ub, wg_ref[...]) + bg_ref[...])))
    merged = (gates[:, :D_MODEL] * _dot(osb_ref[...], wbs_ref[...])
              + gates[:, D_MODEL:] * _dot(odsa_ref[...], wbd_ref[...]))
    h1 = x + _dot(merged.astype(BF16), wo_ref[...])

    u2 = _rms(h1, g2_ref[...]).astype(BF16)
    qb = _dot(u2, wcq_ref[...]).astype(BF16)
    km = km_ref[...]
    vm = vm_ref[...]
    outs = []
    for h in range(MEM_HEADS):
        sl = slice(h * MEM_HEAD_DIM, (h + 1) * MEM_HEAD_DIM)
        logits = _dot_t(qb[:, sl], km[:, sl]) * (MEM_HEAD_DIM ** -0.5)
        m = jnp.max(logits, axis=1, keepdims=True)
        pexp = jnp.exp(logits - m)
        denom = jnp.sum(pexp, axis=1, keepdims=True)
        outs.append((_dot(pexp.astype(BF16), vm[:, sl]) / denom).astype(BF16))
    o = jnp.concatenate(outs, axis=1)
    h_ref[...] = h1 + _dot(o, wco_ref[...])


def _mix_call(x2, osb, odsa, g1, wg, bg, wbs, wbd, wo, g2, wcq, km, vm, wco, seq):
    n = x2.shape[0]
    tm = ROW_TILE
    tiles_per_seq = seq // tm
    half = MEM_HEADS * MEM_HEAD_DIM
    row = lambda i: (i, 0)
    const = lambda i: (0, 0)
    bat = lambda i: (i // tiles_per_seq, 0)

    def full(a):
        return pl.BlockSpec(a.shape, const)

    return pl.pallas_call(
        _mix_kernel,
        grid=(n // tm,),
        in_specs=[
            pl.BlockSpec((tm, D_MODEL), row),
            pl.BlockSpec((tm, 512), row),
            pl.BlockSpec((tm, 512), row),
            full(g1), full(wg), full(bg), full(wbs), full(wbd), full(wo),
            full(g2), full(wcq),
            pl.BlockSpec((N_MEM, half), bat),
            pl.BlockSpec((N_MEM, half), bat),
            full(wco),
        ],
        out_specs=pl.BlockSpec((tm, D_MODEL), row),
        out_shape=jax.ShapeDtypeStruct((n, D_MODEL), F32),
        compiler_params=pltpu.CompilerParams(
            dimension_semantics=("arbitrary",), vmem_limit_bytes=VMEM_LIMIT),
        name="mix",
    )(x2, osb, odsa, g1, wg, bg, wbs, wbd, wo, g2, wcq, km, vm, wco)


def _mlp_kernel(h_ref, g_ref, wu_ref, wd_ref, gf_ref, o_ref, *, final_norm):
    h = h_ref[...]
    ub = _rms(h, g_ref[...]).astype(BF16)
    acc = h
    for c in range(D_FF // D_MODEL):
        sl = slice(c * D_MODEL, (c + 1) * D_MODEL)
        hid = jnp.maximum(_dot(ub, wu_ref[:, sl]), 0.0)
        acc = acc + _dot((hid * hid).astype(BF16), wd_ref[sl, :])
    if final_norm:
        acc = _rms(acc, gf_ref[...])
    o_ref[...] = acc


def _mlp_call(h2, g, wu, wd, gf, final_norm):
    n = h2.shape[0]
    tm = ROW_TILE
    row = lambda i: (i, 0)
    const = lambda i: (0, 0)
    return pl.pallas_call(
        functools.partial(_mlp_kernel, final_norm=final_norm),
        grid=(n // tm,),
        in_specs=[
            pl.BlockSpec((tm, D_MODEL), row),
            pl.BlockSpec((1, D_MODEL), const),
            pl.BlockSpec((D_MODEL, D_FF), const),
            pl.BlockSpec((D_FF, D_MODEL), const),
            pl.BlockSpec((1, D_MODEL), const),
        ],
        out_specs=pl.BlockSpec((tm, D_MODEL), row),
        out_shape=jax.ShapeDtypeStruct((n, D_MODEL), F32),
        compiler_params=pltpu.CompilerParams(
            dimension_semantics=("arbitrary",), vmem_limit_bytes=VMEM_LIMIT),
        name="mlp",
    )(h2, g, wu, wd, gf)


def _pack_w_in(w_in):
    offs = np.cumsum([0, 512, 512, 512, 512, 128, 128, 512, 64, 8])
    sbq, sbk, sbv, dq, dk, dv, iq, ik, iw = [w_in[:, offs[j]:offs[j + 1]] for j in range(9)]
    dq = dq.reshape(D_MODEL, 2, 4, HEAD_DIM).transpose(0, 2, 1, 3).reshape(D_MODEL, 512)
    iw = jnp.pad(iw, ((0, 0), (0, LANES - IDX_HEADS)))
    return jnp.concatenate([sbq, sbk, sbv, dq, dk, dv, iq, ik, ik, iw], axis=1).astype(BF16)


def _rope_tables(seq):
    inv = ROPE_THETA ** (-jnp.arange(0, HEAD_DIM, 2, dtype=F32) / HEAD_DIM)
    ang = jnp.arange(seq).astype(F32)[:, None] * inv[None, :]
    cos = jnp.cos(ang)
    sin = jnp.sin(ang)
    cos_t = jnp.tile(cos, (1, LANES // HALF_ROT))
    sin_t = jnp.tile(jnp.concatenate([-sin, sin], axis=1), (1, LANES // HEAD_DIM))
    return cos_t, sin_t


def kernel(x, mem, norm_mix, w_in, w_branch_sb, w_branch_dsa, w_gate, b_gate, w_out, norm_cross,
           norm_mem, w_cq, w_ckv, w_co, norm_mlp, w_up, w_down, norm_final):
    batch, seq, d = x.shape
    depth = w_in.shape[0]
    cos_t, sin_t = _rope_tables(seq)
    h = x.reshape(batch * seq, d)
    mem2 = mem.reshape(batch * mem.shape[1], d)
    gf = norm_final.reshape(1, d)
    for l in range(depth):
        w_packed = _pack_w_in(w_in[l])
        wbd = (w_branch_dsa[l].reshape(2, 4, HEAD_DIM, d).transpose(1, 0, 2, 3)
               .reshape(DSA_HEADS * HEAD_DIM, d).astype(BF16))
        sbq, sbk, sbv, dq, dk, dv, iq, ik, iw = _proj_call(
            h, norm_mix[l].reshape(1, d), w_packed, cos_t, sin_t, seq)
        o_sb = _sb_call(sbq, sbk, sbv, batch, seq)
        o_dsa = _dsa_call(dq, iq, iw, dk, dv, ik, batch, seq)
        km, vm = _memkv_call(mem2, norm_mem[l].reshape(1, d), w_ckv[l].astype(BF16), batch)
        h2 = _mix_call(h, o_sb, o_dsa, norm_mix[l].reshape(1, d), w_gate[l].astype(BF16),
                       b_gate[l].reshape(1, -1), w_branch_sb[l].astype(BF16), wbd,
                       w_out[l].astype(BF16), norm_cross[l].reshape(1, d), w_cq[l].astype(BF16),
                       km, vm, w_co[l].astype(BF16), seq)
        h = _mlp_call(h2, norm_mlp[l].reshape(1, d), w_up[l].astype(BF16), w_down[l].astype(BF16),
                      gf, final_norm=(l == depth - 1))
    return h.reshape(batch, seq, d)
```

```python
import functools

import jax
import jax.numpy as jnp
import numpy as np
from jax import lax
from jax.experimental import pallas as pl
from jax.experimental.pallas import tpu as pltpu

F32 = jnp.float32
BF16 = jnp.bfloat16

D_MODEL = 1024
N_MEM = 256
SB_HEADS = 8
DSA_HEADS = 8
DSA_KV_HEADS = 2
IDX_HEADS = 8
HEAD_DIM = 64
TOPK_MAX = 256
MEM_HEADS = 4
MEM_HEAD_DIM = 128
D_FF = 4 * D_MODEL
ROPE_THETA = 10000.0
EPS = 1e-6

LANES = 128
HALF_ROT = HEAD_DIM // 2

W_SB = SB_HEADS * HEAD_DIM
OFF_SBQ = 0
OFF_SBK = OFF_SBQ + W_SB
OFF_SBV = OFF_SBK + W_SB
OFF_DQ = OFF_SBV + W_SB
OFF_DK = OFF_DQ + DSA_HEADS * HEAD_DIM
OFF_DV = OFF_DK + DSA_KV_HEADS * HEAD_DIM
OFF_IQ = OFF_DV + DSA_KV_HEADS * HEAD_DIM
OFF_IK = OFF_IQ + IDX_HEADS * HEAD_DIM
OFF_IW = OFF_IK + LANES
D_IN_PACKED = OFF_IW + LANES

ROW_TILE = 512
SB_TILE = 256
DSA_TQ = 256
DSA_KCHUNK = 512
VMEM_LIMIT = 56 * 1024 * 1024
NEG_BIG = -1e30
DSA_BISECT_STEPS = 32
DSA_BISECT_UNROLL = 16
LOG2_E = 1.4426950408889634


def _rms(x, g):
    return x * lax.rsqrt(jnp.mean(x * x, axis=-1, keepdims=True) + EPS) * g


def _dot(a, b):
    return jnp.dot(a, b, preferred_element_type=F32)


def _dot_t(a, b):
    return lax.dot_general(a, b, (((1,), (1,)), ((), ())), preferred_element_type=F32)


def _proj_kernel(x_ref, g_ref, w_ref, cos_ref, sin_ref,
                 sbq_ref, sbk_ref, sbv_ref, dq_ref, dk_ref, dv_ref, iq_ref, ik_ref, iw_ref):
    ub = _rms(x_ref[...], g_ref[...]).astype(BF16)
    cos = cos_ref[...]
    sin = sin_ref[...]
    lane = lax.broadcasted_iota(jnp.int32, (1, LANES), 1)
    first_half = (lane % HEAD_DIM) < HALF_ROT

    def mm(off, width=LANES):
        return _dot(ub, w_ref[:, off:off + width])

    def rope(s):
        rot = jnp.where(first_half, pltpu.roll(s, LANES - HALF_ROT, 1), pltpu.roll(s, HALF_ROT, 1))
        return s * cos + rot * sin

    q_scale = HEAD_DIM ** -0.5
    sbq_ref[...] = (mm(OFF_SBQ, W_SB) * (q_scale * LOG2_E)).astype(BF16)
    sbk_ref[...] = mm(OFF_SBK, W_SB).astype(BF16)
    sbv_ref[...] = mm(OFF_SBV, W_SB).astype(BF16)
    dq = mm(OFF_DQ, 4 * LANES)
    iq = mm(OFF_IQ, 4 * LANES)
    for p in range(4):
        sl = slice(p * LANES, (p + 1) * LANES)
        dq_ref[:, sl] = (rope(dq[:, sl]) * (q_scale * LOG2_E)).astype(BF16)
        iq_ref[:, sl] = (rope(iq[:, sl]) * q_scale).astype(BF16)
    dkv = mm(OFF_DK, 2 * LANES)
    dk_ref[...] = rope(dkv[:, :LANES]).astype(BF16)
    dv_ref[...] = dkv[:, LANES:].astype(BF16)
    ikw = mm(OFF_IK, 2 * LANES)
    ik_ref[...] = rope(ikw[:, :LANES]).astype(BF16)
    iw_ref[...] = ikw[:, LANES:] * (IDX_HEADS ** -0.5)


def _proj_call(x2, g, w_packed, cos_t, sin_t, seq):
    n = x2.shape[0]
    tm = ROW_TILE
    tiles_per_seq = seq // tm
    row = lambda i: (i, 0)
    const = lambda i: (0, 0)
    pos = lambda i: (i % tiles_per_seq, 0)
    out_w = [W_SB, W_SB, W_SB, 512, LANES, LANES, 512, LANES, LANES]
    out_dt = [BF16] * 8 + [F32]
    return pl.pallas_call(
        _proj_kernel,
        grid=(n // tm,),
        in_specs=[
            pl.BlockSpec((tm, D_MODEL), row),
            pl.BlockSpec((1, D_MODEL), const),
            pl.BlockSpec((D_MODEL, D_IN_PACKED), const),
            pl.BlockSpec((tm, LANES), pos),
            pl.BlockSpec((tm, LANES), pos),
        ],
        out_specs=[pl.BlockSpec((tm, w), row) for w in out_w],
        out_shape=[jax.ShapeDtypeStruct((n, w), dt) for w, dt in zip(out_w, out_dt)],
        compiler_params=pltpu.CompilerParams(
            dimension_semantics=("arbitrary",), vmem_limit_bytes=VMEM_LIMIT),
        name="proj",
    )(x2, g, w_packed, cos_t, sin_t)


def _sb_kernel(q_ref, k_ref, v_ref, o_ref, acc_sc):
    t = SB_TILE
    i = pl.program_id(1)
    lane = lax.broadcasted_iota(jnp.int32, (1, LANES), 1)
    first_head = lane < HEAD_DIM
    row = lax.broadcasted_iota(jnp.int32, (t, t), 0)
    col = lax.broadcasted_iota(jnp.int32, (t, t), 1)
    strict = col < row
    later = jnp.where(row > col, 1.0, 0.0).astype(BF16)
    zero_q = jnp.zeros((t, LANES), BF16)
    pairs = SB_HEADS // 2
    strict2 = jnp.concatenate([strict, strict], axis=0)
    qp = []
    for p in range(pairs):
        slab = q_ref[:, p * LANES:(p + 1) * LANES]
        qp.append(jnp.concatenate(
            [jnp.where(first_head, slab, zero_q), jnp.where(first_head, zero_q, slab)], axis=0))

    def tile(start, diag, carries):
        new = []
        for group in (range(0, pairs // 2), range(pairs // 2, pairs)):
            z = {p: _dot_t(qp[p], k_ref[pl.ds(start, t), p * LANES:(p + 1) * LANES]) for p in group}
            log_beta, neg_log_1mb = {}, {}
            for p in group:
                z_neg = jnp.minimum(z[p], 0.0)
                z_pos = z[p] - z_neg
                sp = jnp.log2(1.0 + jnp.exp2(z_neg - z_pos))
                log_beta[p] = z_neg - sp
                nl = z_pos + sp
                neg_log_1mb[p] = jnp.where(strict2, nl, 0.0) if diag else nl
            within = {p: _dot(neg_log_1mb[p].astype(BF16), later) for p in group}
            a = {}
            for p in group:
                ap = jnp.exp2(log_beta[p] - (within[p] + carries[p]))
                a[p] = (jnp.where(strict2, ap, 0.0) if diag else ap).astype(BF16)
            av = {p: _dot(a[p], v_ref[pl.ds(start, t), p * LANES:(p + 1) * LANES]) for p in group}
            for p in group:
                sl = slice(p * LANES, (p + 1) * LANES)
                contrib = jnp.where(first_head, av[p][:t], av[p][t:])
                if diag:
                    acc_sc[:, sl] = contrib
                else:
                    acc_sc[:, sl] += contrib
                new.append(carries[p] + (within[p][:, 0:1] + neg_log_1mb[p][:, 0:1]))
        return tuple(new)

    zc = jnp.zeros((2 * t, 1), F32)
    carries = tile(pl.multiple_of(i * t, t), True, (zc,) * pairs)

    def body(s, carries):
        return tile(pl.multiple_of((i - 1 - s) * t, t), False, carries)

    lax.fori_loop(0, i, body, carries)
    o_ref[...] = acc_sc[...].astype(BF16)


def _sb_call(q, k, v, batch, seq):
    t = SB_TILE
    nq = seq // t
    qmap = lambda b, i: (b * nq + i, 0)
    kvmap = lambda b, i: (b, 0)
    return pl.pallas_call(
        _sb_kernel,
        grid=(batch, nq),
        in_specs=[
            pl.BlockSpec((t, W_SB), qmap),
            pl.BlockSpec((seq, W_SB), kvmap),
            pl.BlockSpec((seq, W_SB), kvmap),
        ],
        out_specs=pl.BlockSpec((t, W_SB), qmap),
        out_shape=jax.ShapeDtypeStruct((batch * seq, W_SB), BF16),
        scratch_shapes=[pltpu.VMEM((t, W_SB), F32)],
        compiler_params=pltpu.CompilerParams(
            dimension_semantics=("arbitrary", "arbitrary"), vmem_limit_bytes=VMEM_LIMIT),
        name="stickbreak",
    )(q, k, v)


def _dsa_tile(width, i, dq_ref, iq_ref, iw_ref, dk_ref, dv_ref, ik_ref, o_ref, score_sc, bias_sc):
    tq = DSA_TQ
    lane = lax.broadcasted_iota(jnp.int32, (1, LANES), 1)
    half = lane // HEAD_DIM
    t_idx = i * tq + lax.broadcasted_iota(jnp.int32, (tq, 1), 0)

    iq = iq_ref[...]
    zero_q = jnp.zeros((tq, LANES), BF16)
    qstack = jnp.concatenate(
        [jnp.where(half == (h % 2), iq[:, (h // 2) * LANES:(h // 2 + 1) * LANES], zero_q)
         for h in range(IDX_HEADS)], axis=0)
    iw = iw_ref[...]
    wcol = [iw[:, h:h + 1] for h in range(IDX_HEADS)]
    lo = hi = None
    for c in range(width // DSA_KCHUNK):
        ksl = slice(c * DSA_KCHUNK, (c + 1) * DSA_KCHUNK)
        dots = _dot_t(qstack, ik_ref[ksl, :])
        score = jnp.zeros((tq, DSA_KCHUNK), F32)
        for h in range(IDX_HEADS):
            score = score + jnp.maximum(dots[h * tq:(h + 1) * tq], 0.0) * wcol[h]
        kpos = c * DSA_KCHUNK + lax.broadcasted_iota(jnp.int32, (1, DSA_KCHUNK), 1)
        causal = kpos <= t_idx
        score_sc[:, ksl] = jnp.where(causal, score, -jnp.inf)
        cmin = jnp.min(jnp.where(causal, score, jnp.inf), axis=1, keepdims=True)
        cmax = jnp.max(jnp.where(causal, score, -jnp.inf), axis=1, keepdims=True)
        lo = cmin if c == 0 else jnp.minimum(lo, cmin)
        hi = cmax if c == 0 else jnp.maximum(hi, cmax)

    k_row = jnp.minimum(t_idx + 1, TOPK_MAX).astype(F32)
    hi = hi + (jnp.abs(hi) + 1.0)

    def count(pred):
        return jnp.sum(jnp.where(pred, 1.0, 0.0), axis=1, keepdims=True)

    def bisect(it, bracket):
        lo, hi = bracket
        mid = lo + 0.5 * (hi - lo)
        ge = count(score_sc[:, :width] >= mid) >= k_row
        return jnp.where(ge, mid, lo), jnp.where(ge, hi, mid)

    lo, hi = lax.fori_loop(0, DSA_BISECT_STEPS, bisect, (lo, hi), unroll=DSA_BISECT_UNROLL)

    surplus = jnp.max(count(score_sc[:, :width] >= lo) - k_row)

    @pl.when(surplus <= 0.0)
    def _():
        bias_sc[:, :width] = jnp.where(score_sc[:, :width] >= lo, 0.0, NEG_BIG)

    @pl.when(surplus > 0.0)
    def _():
        lo_f, hi_f = lo, hi
        need = k_row - count(score_sc[:, :width] >= hi_f)
        r128 = lax.broadcasted_iota(jnp.int32, (LANES, LANES), 0)
        c128 = lax.broadcasted_iota(jnp.int32, (LANES, LANES), 1)
        earlier = jnp.where(r128 < c128, 1.0, 0.0).astype(BF16)
        ties_before = jnp.zeros((tq, 1), F32)
        for c in range(width // LANES):
            ksl = slice(c * LANES, (c + 1) * LANES)
            sc = score_sc[:, ksl]
            tied = jnp.where(sc >= lo_f, jnp.where(sc < hi_f, 1.0, 0.0), 0.0)
            rank = _dot(tied.astype(BF16), earlier) + ties_before
            tie_bias = jnp.where(rank < need, 0.0, NEG_BIG)
            bias_sc[:, ksl] = jnp.where(sc >= hi_f, 0.0, jnp.where(sc >= lo_f, tie_bias, NEG_BIG))
            ties_before = ties_before + jnp.sum(tied, axis=1, keepdims=True)

    dq = dq_ref[...]
    heads_per_group = DSA_HEADS // DSA_KV_HEADS
    outs = []
    for g in range(DSA_KV_HEADS):
        qg = jnp.concatenate(
            [jnp.where(half == g, dq[:, p * LANES:(p + 1) * LANES], zero_q)
             for p in range(heads_per_group)], axis=0)
        m = denom = acc = None
        for c in range(width // DSA_KCHUNK):
            ksl = slice(c * DSA_KCHUNK, (c + 1) * DSA_KCHUNK)
            logits = _dot_t(qg, dk_ref[ksl, :])
            logits = jnp.concatenate(
                [logits[p * tq:(p + 1) * tq] + bias_sc[:, ksl] for p in range(heads_per_group)], axis=0)
            cmax = jnp.max(logits, axis=1, keepdims=True)
            if c == 0:
                m = cmax
                pexp = jnp.exp2(logits - m)
                denom = jnp.sum(pexp, axis=1, keepdims=True)
                acc = _dot(pexp.astype(BF16), dv_ref[ksl, :])
            else:
                m_new = jnp.maximum(m, cmax)
                alpha = jnp.exp2(m - m_new)
                pexp = jnp.exp2(logits - m_new)
                denom = alpha * denom + jnp.sum(pexp, axis=1, keepdims=True)
                acc = alpha * acc + _dot(pexp.astype(BF16), dv_ref[ksl, :])
                m = m_new
        outs.append(acc / denom)
    for p in range(heads_per_group):
        rows = slice(p * tq, (p + 1) * tq)
        o_ref[:, p * LANES:(p + 1) * LANES] = jnp.where(half == 0, outs[0][rows], outs[1][rows]).astype(BF16)


def _dsa_kernel(dq_ref, iq_ref, iw_ref, dk_ref, dv_ref, ik_ref, o_ref, score_sc, bias_sc, *, seq):
    i = pl.program_id(1)
    tiles_per_width = DSA_KCHUNK // DSA_TQ
    for v in range(seq // DSA_KCHUNK):
        @pl.when(i // tiles_per_width == v)
        def _(v=v):
            _dsa_tile((v + 1) * DSA_KCHUNK, i, dq_ref, iq_ref, iw_ref, dk_ref, dv_ref, ik_ref,
                      o_ref, score_sc, bias_sc)


def _dsa_call(dq, iq, iw, dk, dv, ik, batch, seq):
    tq = DSA_TQ
    nq = seq // tq
    qmap = lambda b, i: (b * nq + i, 0)
    kvmap = lambda b, i: (b, 0)
    return pl.pallas_call(
        functools.partial(_dsa_kernel, seq=seq),
        grid=(batch, nq),
        in_specs=[
            pl.BlockSpec((tq, 512), qmap),
            pl.BlockSpec((tq, 512), qmap),
            pl.BlockSpec((tq, LANES), qmap),
            pl.BlockSpec((seq, LANES), kvmap),
            pl.BlockSpec((seq, LANES), kvmap),
            pl.BlockSpec((seq, LANES), kvmap),
        ],
        out_specs=pl.BlockSpec((tq, 512), qmap),
        out_shape=jax.ShapeDtypeStruct((batch * seq, 512), BF16),
        scratch_shapes=[pltpu.VMEM((tq, seq), F32), pltpu.VMEM((tq, seq), F32)],
        compiler_params=pltpu.CompilerParams(
            dimension_semantics=("arbitrary", "arbitrary"), vmem_limit_bytes=VMEM_LIMIT),
        name="dsa",
    )(dq, iq, iw, dk, dv, ik)


def _memkv_kernel(m_ref, g_ref, w_ref, k_ref, v_ref):
    mb = _rms(m_ref[...], g_ref[...]).astype(BF16)
    kv = _dot(mb, w_ref[...])
    half = MEM_HEADS * MEM_HEAD_DIM
    k_ref[...] = kv[:, :half].astype(BF16)
    v_ref[...] = kv[:, half:].astype(BF16)


def _memkv_call(mem2, g, w_ckv, batch):
    half = MEM_HEADS * MEM_HEAD_DIM
    row = lambda b: (b, 0)
    const = lambda b: (0, 0)
    return pl.pallas_call(
        _memkv_kernel,
        grid=(batch,),
        in_specs=[
            pl.BlockSpec((N_MEM, D_MODEL), row),
            pl.BlockSpec((1, D_MODEL), const),
            pl.BlockSpec((D_MODEL, 2 * half), const),
        ],
        out_specs=[pl.BlockSpec((N_MEM, half), row)] * 2,
        out_shape=[jax.ShapeDtypeStruct((batch * N_MEM, half), BF16)] * 2,
        compiler_params=pltpu.CompilerParams(
            dimension_semantics=("arbitrary",), vmem_limit_bytes=VMEM_LIMIT),
        name="memkv",
    )(mem2, g, w_ckv)


def _mix_kernel(x_ref, osb_ref, odsa_ref, g1_ref, wg_ref, bg_ref, wbs_ref, wbd_ref, wo_ref,
                g2_ref, wcq_ref, km_ref, vm_ref, wco_ref, h_ref):
    x = x_ref[...]
    ub = _rms(x, g1_ref[...]).astype(BF16)
    gates = 1.0 / (1.0 + jnp.exp(-(_dot(ub, wg_ref[...]) + bg_ref[...])))
    merged = (gates[:, :D_MODEL] * _dot(osb_ref[...], wbs_ref[...])
              + gates[:, D_MODEL:] * _dot(odsa_ref[...], wbd_ref[...]))
    h1 = x + _dot(merged.astype(BF16), wo_ref[...])

    u2 = _rms(h1, g2_ref[...]).astype(BF16)
    qb = _dot(u2, wcq_ref[...]).astype(BF16)
    km = km_ref[...]
    vm = vm_ref[...]
    outs = []
    for h in range(MEM_HEADS):
        sl = slice(h * MEM_HEAD_DIM, (h + 1) * MEM_HEAD_DIM)
        logits = _dot_t(qb[:, sl], km[:, sl]) * (MEM_HEAD_DIM ** -0.5)
        m = jnp.max(logits, axis=1, keepdims=True)
        pexp = jnp.exp(logits - m)
        denom = jnp.sum(pexp, axis=1, keepdims=True)
        outs.append((_dot(pexp.astype(BF16), vm[:, sl]) / denom).astype(BF16))
    o = jnp.concatenate(outs, axis=1)
    h_ref[...] = h1 + _dot(o, wco_ref[...])


def _mix_call(x2, osb, odsa, g1, wg, bg, wbs, wbd, wo, g2, wcq, km, vm, wco, seq):
    n = x2.shape[0]
    tm = ROW_TILE
    tiles_per_seq = seq // tm
    half = MEM_HEADS * MEM_HEAD_DIM
    row = lambda i: (i, 0)
    const = lambda i: (0, 0)
    bat = lambda i: (i // tiles_per_seq, 0)

    def full(a):
        return pl.BlockSpec(a.shape, const)

    return pl.pallas_call(
        _mix_kernel,
        grid=(n // tm,),
        in_specs=[
            pl.BlockSpec((tm, D_MODEL), row),
            pl.BlockSpec((tm, 512), row),
            pl.BlockSpec((tm, 512), row),
            full(g1), full(wg), full(bg), full(wbs), full(wbd), full(wo),
            full(g2), full(wcq),
            pl.BlockSpec((N_MEM, half), bat),
            pl.BlockSpec((N_MEM, half), bat),
            full(wco),
        ],
        out_specs=pl.BlockSpec((tm, D_MODEL), row),
        out_shape=jax.ShapeDtypeStruct((n, D_MODEL), F32),
        compiler_params=pltpu.CompilerParams(
            dimension_semantics=("arbitrary",), vmem_limit_bytes=VMEM_LIMIT),
        name="mix",
    )(x2, osb, odsa, g1, wg, bg, wbs, wbd, wo, g2, wcq, km, vm, wco)


def _mlp_kernel(h_ref, g_ref, wu_ref, wd_ref, gf_ref, o_ref, *, final_norm):
    h = h_ref[...]
    ub = _rms(h, g_ref[...]).astype(BF16)
    acc = h
    for c in range(D_FF // D_MODEL):
        sl = slice(c * D_MODEL, (c + 1) * D_MODEL)
        hid = jnp.maximum(_dot(ub, wu_ref[:, sl]), 0.0)
        acc = acc + _dot((hid * hid).astype(BF16), wd_ref[sl, :])
    if final_norm:
        acc = _rms(acc, gf_ref[...])
    o_ref[...] = acc


def _mlp_call(h2, g, wu, wd, gf, final_norm):
    n = h2.shape[0]
    tm = ROW_TILE
    row = lambda i: (i, 0)
    const = lambda i: (0, 0)
    return pl.pallas_call(
        functools.partial(_mlp_kernel, final_norm=final_norm),
        grid=(n // tm,),
        in_specs=[
            pl.BlockSpec((tm, D_MODEL), row),
            pl.BlockSpec((1, D_MODEL), const),
            pl.BlockSpec((D_MODEL, D_FF), const),
            pl.BlockSpec((D_FF, D_MODEL), const),
            pl.BlockSpec((1, D_MODEL), const),
        ],
        out_specs=pl.BlockSpec((tm, D_MODEL), row),
        out_shape=jax.ShapeDtypeStruct((n, D_MODEL), F32),
        compiler_params=pltpu.CompilerParams(
            dimension_semantics=("arbitrary",), vmem_limit_bytes=VMEM_LIMIT),
        name="mlp",
    )(h2, g, wu, wd, gf)


def _pack_w_in(w_in):
    offs = np.cumsum([0, 512, 512, 512, 512, 128, 128, 512, 64, 8])
    sbq, sbk, sbv, dq, dk, dv, iq, ik, iw = [w_in[:, offs[j]:offs[j + 1]] for j in range(9)]
    dq = dq.reshape(D_MODEL, 2, 4, HEAD_DIM).transpose(0, 2, 1, 3).reshape(D_MODEL, 512)
    iw = jnp.pad(iw, ((0, 0), (0, LANES - IDX_HEADS)))
    return jnp.concatenate([sbq, sbk, sbv, dq, dk, dv, iq, ik, ik, iw], axis=1).astype(BF16)


def _rope_tables(seq):
    inv = ROPE_THETA ** (-jnp.arange(0, HEAD_DIM, 2, dtype=F32) / HEAD_DIM)
    ang = jnp.arange(seq).astype(F32)[:, None] * inv[None, :]
    cos = jnp.cos(ang)
    sin = jnp.sin(ang)
    cos_t = jnp.tile(cos, (1, LANES // HALF_ROT))
    sin_t = jnp.tile(jnp.concatenate([-sin, sin], axis=1), (1, LANES // HEAD_DIM))
    return cos_t, sin_t


def kernel(x, mem, norm_mix, w_in, w_branch_sb, w_branch_dsa, w_gate, b_gate, w_out, norm_cross,
           norm_mem, w_cq, w_ckv, w_co, norm_mlp, w_up, w_down, norm_final):
    batch, seq, d = x.shape
    depth = w_in.shape[0]
    cos_t, sin_t = _rope_tables(seq)
    h = x.reshape(batch * seq, d)
    mem2 = mem.reshape(batch * mem.shape[1], d)
    gf = norm_final.reshape(1, d)
    for l in range(depth):
        w_packed = _pack_w_in(w_in[l])
        wbd = (w_branch_dsa[l].reshape(2, 4, HEAD_DIM, d).transpose(1, 0, 2, 3)
               .reshape(DSA_HEADS * HEAD_DIM, d).astype(BF16))
        sbq, sbk, sbv, dq, dk, dv, iq, ik, iw = _proj_call(
            h, norm_mix[l].reshape(1, d), w_packed, cos_t, sin_t, seq)
        o_sb = _sb_call(sbq, sbk, sbv, batch, seq)
        o_dsa = _dsa_call(dq, iq, iw, dk, dv, ik, batch, seq)
        km, vm = _memkv_call(mem2, norm_mem[l].reshape(1, d), w_ckv[l].astype(BF16), batch)
        h2 = _mix_call(h, o_sb, o_dsa, norm_mix[l].reshape(1, d), w_gate[l].astype(BF16),
                       b_gate[l].reshape(1, -1), w_branch_sb[l].astype(BF16), wbd,
                       w_out[l].astype(BF16), norm_cross[l].reshape(1, d), w_cq[l].astype(BF16),
                       km, vm, w_co[l].astype(BF16), seq)
        h = _mlp_call(h2, norm_mlp[l].reshape(1, d), w_up[l].astype(BF16), w_down[l].astype(BF16),
                      gf, final_norm=(l == depth - 1))
    return h.reshape(batch, seq, d)
```
